```python
import math
import jax, jax.numpy as jnp
from jax import lax
import numpy as np

D_MODEL = 2048
BATCH = 32
SEQ = 256
DEPTH = 2
DEC_BATCH = 8
DEC_SEQ = 2048
PAST_LEN = 256

GRID_W = 64
N_EVEN = (DEPTH + 1) // 2
N_ODD = DEPTH // 2
MOD_COUNT = 6
EPS = 1e-6

GLA_HEADS = 4
GLA_DK = D_MODEL // 2 // GLA_HEADS
GLA_DV = D_MODEL // GLA_HEADS
GLA_RANK = 16
GLA_TAU = 16.0
GLA_CHUNK = 64
ROPE_BASE = 10000.0
GLA_QK = GLA_HEADS * GLA_DK
GLA_V = GLA_HEADS * GLA_DV

RNN_WIDTH = D_MODEL
RNN_BLOCKS = 16
RNN_BLOCK = RNN_WIDTH // RNN_BLOCKS
RNN_CONV = 4
RNN_C = 8.0

SSD_INNER = 2 * D_MODEL
SSD_HEAD_DIM = 64
SSD_HEADS = SSD_INNER // SSD_HEAD_DIM
SSD_STATE = 128
SSD_GROUPS = 8
SSD_CONV = 4
SSD_CHUNK = 128
SSD_XBC = SSD_INNER + 2 * SSD_GROUPS * SSD_STATE

D_FF = (((8 * D_MODEL + 2) // 3 + 255) // 256) * 256

EVEN_SPLITS = (GLA_QK, 2 * GLA_QK, 2 * GLA_QK + GLA_V, 2 * GLA_QK + 2 * GLA_V,
               2 * GLA_QK + 2 * GLA_V + 2 * GLA_RANK, 2 * GLA_QK + 2 * GLA_V + 2 * GLA_RANK + RNN_WIDTH)
EVEN_PROJ = EVEN_SPLITS[-1] + RNN_WIDTH
ODD_PROJ = SSD_INNER + SSD_XBC + 2 * SSD_HEADS

kernel_name = "hybrid_gla_rglru_ssd_prefix_diffusion_step"


def rmsnorm(x, g):
    xf = x.astype(jnp.float32)
    y = xf * lax.rsqrt(jnp.mean(xf * xf, axis=-1, keepdims=True) + EPS)
    return (y * g.astype(jnp.float32)).astype(x.dtype)


def flip(t):
    return jnp.flip(t, axis=1)


def adaln(cvec, w, b):
    return (jax.nn.silu(cvec) @ w + b).reshape(cvec.shape[0], MOD_COUNT, D_MODEL)


def modulated_input(x, mod, j, g):
    return rmsnorm(x, g) * (1.0 + mod[:, 3 * j + 1, None]) + mod[:, 3 * j, None]


def gated_residual(x, out, mod, j, g):
    return x + mod[:, 3 * j + 2, None] * rmsnorm(out, g)


def dwconv_centred(x, w, b):
    K, C = w.shape
    left = K // 2
    y = lax.conv_general_dilated(x, w[:, None, :].astype(x.dtype), window_strides=(1,),
                                 padding=[(left, K - 1 - left)],
                                 dimension_numbers=('NWC', 'WIO', 'NWC'), feature_group_count=C)
    return y + b.astype(x.dtype)


def _rotate(x, pos):
    nf = x.shape[-1] // 2
    inv = ROPE_BASE ** (-jnp.arange(nf, dtype=jnp.float32) / nf)
    ang = pos.astype(jnp.float32)[:, None] * inv
    cos = jnp.cos(ang)[None, :, None, :]
    sin = jnp.sin(ang)[None, :, None, :]
    x1, x2 = x[..., :nf], x[..., nf:]
    return jnp.concatenate([x1 * cos - x2 * sin, x1 * sin + x2 * cos], axis=-1)


def rope_2d(x, rows):
    row = jnp.repeat(jnp.arange(rows), GRID_W)
    col = jnp.tile(jnp.arange(GRID_W), rows)
    half = x.shape[-1] // 2
    return jnp.concatenate([_rotate(x[..., :half], row), _rotate(x[..., half:], col)], axis=-1)


def gla_scan(q, k, v, log_a, s0):
    Bsz, L, H, DK = q.shape
    DV = v.shape[-1]
    nc = L // GLA_CHUNK
    tril = jnp.tril(jnp.ones((GLA_CHUNK, GLA_CHUNK), dtype=bool))

    def chunks(t):
        return jnp.moveaxis(t.reshape(Bsz, nc, GLA_CHUNK, H, t.shape[-1]), 1, 0)

    def step(S, inp):
        qc, kc, vc, ac = inp
        b = jnp.cumsum(ac, axis=1)
        b_last = b[:, -1]
        q_dec = qc * jnp.exp(b)
        scores = jnp.einsum('bthk,bshk->bhts', q_dec, kc * jnp.exp(-b))
        scores = jnp.where(tril, scores, 0.0)
        o = jnp.einsum('bhts,bshv->bthv', scores, vc) + jnp.einsum('bthk,bhkv->bthv', q_dec, S)
        S = jnp.exp(b_last)[..., None] * S + jnp.einsum('bshk,bshv->bhkv', kc * jnp.exp(b_last[:, None] - b), vc)
        return S, o

    S, o = lax.scan(step, s0, tuple(chunks(t) for t in (q, k, v, log_a)))
    return jnp.moveaxis(o, 0, 1).reshape(Bsz, L, H, DV), S


def linear_scan(a, u, h0):
    def comb(x, y):
        return x[0] * y[0], y[0] * x[1] + y[1]
    A, Bc = lax.associative_scan(comb, (a, u), axis=1)
    h = Bc + A * h0[:, None]
    return h, h[:, -1]


def ssd_scan(x, dt, A, Bm, Cm, s0):
    Bsz, L, H, P = x.shape
    G, N = Bm.shape[2], Bm.shape[3]
    HG = H // G
    nc = L // SSD_CHUNK
    xdt = (x * dt[..., None]).reshape(Bsz, nc, SSD_CHUNK, G, HG, P)
    la = (dt * A).reshape(Bsz, nc, SSD_CHUNK, G, HG)
    Bc = Bm.reshape(Bsz, nc, SSD_CHUNK, G, N)
    Cc = Cm.reshape(Bsz, nc, SSD_CHUNK, G, N)
    tril = jnp.tril(jnp.ones((SSD_CHUNK, SSD_CHUNK), dtype=bool))

    def step(S, inp):
        xdt_c, la_c, B_c, C_c = inp
        cum = jnp.cumsum(la_c, axis=1)
        seg = cum[:, :, None] - cum[:, None, :]
        decay = jnp.exp(jnp.where(tril[None, :, :, None, None], seg, -jnp.inf))
        cb = jnp.einsum('btgn,bsgn->btsg', C_c, B_c)
        y = jnp.einsum('btsg,btsgj,bsgjp->btgjp', cb, decay, xdt_c)
        y = y + jnp.einsum('btgn,bgjpn->btgjp', C_c, S) * jnp.exp(cum)[..., None]
        to_end = jnp.exp(cum[:, -1:] - cum)
        S = jnp.exp(cum[:, -1])[..., None, None] * S + jnp.einsum('bsgn,bsgj,bsgjp->bgjpn', B_c, to_end, xdt_c)
        return S, y

    xs = tuple(jnp.moveaxis(t, 1, 0) for t in (xdt, la, Bc, Cc))
    S, y = lax.scan(step, s0.reshape(Bsz, G, HG, P, N), xs)
    return jnp.moveaxis(y, 0, 1).reshape(Bsz, L, H, P), S.reshape(Bsz, H, P, N)


def even_mixer(h, w_in, w_out, gla_w_up, gla_b_up, gla_norm_g, conv_w, conv_b, w_r, b_r, w_i, b_i, lam,
               gla_s0, rnn_h0, rows):
    f32 = jnp.float32
    Bsz, L, _ = h.shape
    q, k, v, g, lr, xr, yr = jnp.split(h @ w_in, EVEN_SPLITS, axis=-1)
    q = q.astype(f32).reshape(Bsz, L, GLA_HEADS, GLA_DK) * (GLA_DK ** -0.5)
    k = k.astype(f32).reshape(Bsz, L, GLA_HEADS, GLA_DK)
    if rows is not None:
        q = rope_2d(q, rows)
        k = rope_2d(k, rows)
    v = v.astype(f32).reshape(Bsz, L, GLA_HEADS, GLA_DV)
    lr = lr.astype(f32).reshape(Bsz, L, 2, GLA_RANK)
    log_a = jax.nn.log_sigmoid(jnp.einsum('bldr,drk->bldk', lr, gla_w_up.astype(f32)) + gla_b_up.astype(f32)) / GLA_TAU
    log_a = log_a.reshape(Bsz, L, 2, GLA_HEADS, GLA_DK)
    s0 = gla_s0.astype(f32)
    o_f, s_f = gla_scan(q, k, v, log_a[:, :, 0], s0[:, 0])
    o_b, s_b = gla_scan(flip(q), flip(k), flip(v), flip(log_a[:, :, 1]), s0[:, 1])
    o = rmsnorm(o_f + flip(o_b), gla_norm_g.reshape(GLA_HEADS, GLA_DV))
    o = o.reshape(Bsz, L, GLA_V) * jax.nn.silu(g.astype(f32))
    xc = dwconv_centred(xr, conv_w, conv_b).astype(f32)
    xb = xc.reshape(Bsz, L, RNN_BLOCKS, RNN_BLOCK)
    r = jax.nn.sigmoid(jnp.einsum('blni,dnij->bldnj', xb, w_r.astype(f32)).reshape(Bsz, L, 2, RNN_WIDTH) + b_r.astype(f32))
    i = jax.nn.sigmoid(jnp.einsum('blni,dnij->bldnj', xb, w_i.astype(f32)).reshape(Bsz, L, 2, RNN_WIDTH) + b_i.astype(f32))
    log_ar = RNN_C * r * jax.nn.log_sigmoid(lam.astype(f32))
    a = jnp.exp(log_ar)
    u = jnp.sqrt(-jnp.expm1(2.0 * log_ar)) * i * xc[:, :, None]
    h0 = rnn_h0.astype(f32)
    h_f, hl_f = linear_scan(a[:, :, 0], u[:, :, 0], h0[:, 0])
    h_b, hl_b = linear_scan(flip(a[:, :, 1]), flip(u[:, :, 1]), h0[:, 1])
    y_rnn = (h_f + flip(h_b)) * jax.nn.gelu(yr.astype(f32))
    out = jnp.concatenate([o, y_rnn], axis=-1).astype(h.dtype) @ w_out
    return out, jnp.stack([s_f, s_b], axis=1), jnp.stack([hl_f, hl_b], axis=1)


def odd_mixer(h, w_in, w_out, conv_w, conv_b, dt_bias, a_log, d_skip, norm_g, s0):
    f32 = jnp.float32
    Bsz, L, _ = h.shape
    z, xbc, dt_raw = jnp.split(h @ w_in, [SSD_INNER, SSD_INNER + SSD_XBC], axis=-1)
    xbc = jax.nn.silu(dwconv_centred(xbc, conv_w, conv_b).astype(f32))
    x, Bm, Cm = jnp.split(xbc, [SSD_INNER, SSD_INNER + SSD_GROUPS * SSD_STATE], axis=-1)
    x = x.reshape(Bsz, L, SSD_HEADS, SSD_HEAD_DIM)
    Bm = Bm.reshape(Bsz, L, SSD_GROUPS, SSD_STATE)
    Cm = Cm.reshape(Bsz, L, SSD_GROUPS, SSD_STATE)
    dt = jax.nn.softplus(dt_raw.astype(f32).reshape(Bsz, L, 2, SSD_HEADS) + dt_bias.astype(f32))
    A = -jnp.exp(a_log.astype(f32))
    s0 = s0.astype(f32)
    y_f, s_f = ssd_scan(x, dt[:, :, 0], A[0], Bm, Cm, s0[:, 0])
    y_b, s_b = ssd_scan(flip(x), flip(dt[:, :, 1]), A[1], flip(Bm), flip(Cm), s0[:, 1])
    y = y_f + flip(y_b) + d_skip.astype(f32)[:, None] * x
    y = y.reshape(Bsz, L, SSD_INNER) * jax.nn.silu(z.astype(f32))
    y = rmsnorm(y.reshape(Bsz, L, SSD_GROUPS, SSD_INNER // SSD_GROUPS),
                norm_g.reshape(SSD_GROUPS, SSD_INNER // SSD_GROUPS)).reshape(Bsz, L, SSD_INNER)
    return y.astype(h.dtype) @ w_out, jnp.stack([s_f, s_b], axis=1)


def swiglu(h, w_gate, w_up, w_down):
    return (jax.nn.silu(h @ w_gate) * (h @ w_up)) @ w_down


def setup_inputs(seed: int = 0) -> dict:
    key = jax.random.key(seed)
    ks = iter(jax.random.split(key, 48))
    f32 = jnp.float32

    def nrm(shape, scale):
        return scale * jax.random.normal(next(ks), shape, f32)

    def unif(shape, lo, hi):
        return jax.random.uniform(next(ks), shape, f32, lo, hi)

    u = unif((N_EVEN, 2, RNN_WIDTH), 0.9, 0.999)
    s = u ** (1.0 / RNN_C)
    rnn_lam = jnp.log(s) - jnp.log1p(-s)
    dt0 = jnp.exp(unif((N_ODD, 2, SSD_HEADS), math.log(1e-3), math.log(1e-1)))
    ssd_dt_bias = dt0 + jnp.log(-jnp.expm1(-dt0))
    return {
        'x_prompt': nrm((BATCH, SEQ, D_MODEL), 1.0),
        'x_sample': nrm((DEC_BATCH, DEC_SEQ, D_MODEL), 1.0),
        'state_gla': nrm((DEC_BATCH, N_EVEN, 2, GLA_HEADS, GLA_DK, GLA_DV), 0.1),
        'state_rglru': nrm((DEC_BATCH, N_EVEN, 2, RNN_WIDTH), 0.5),
        'state_ssd': nrm((DEC_BATCH, N_ODD, 2, SSD_HEADS, SSD_HEAD_DIM, SSD_STATE), 0.1),
        'c': nrm((DEC_BATCH, D_MODEL), 1.0),
        'c_ctx': nrm((D_MODEL,), 1.0),
        'w_ada': nrm((DEPTH, D_MODEL, MOD_COUNT * D_MODEL), 0.5 * D_MODEL ** -0.5),
        'b_ada': nrm((DEPTH, MOD_COUNT * D_MODEL), 0.01),
        'norm_g': 1.0 + nrm((DEPTH, 4, D_MODEL), 0.05),
        'ev_w_in': nrm((N_EVEN, D_MODEL, EVEN_PROJ), D_MODEL ** -0.5),
        'ev_w_out': nrm((N_EVEN, GLA_V + RNN_WIDTH, D_MODEL), (GLA_V + RNN_WIDTH) ** -0.5),
        'gla_w_up': nrm((N_EVEN, 2, GLA_RANK, GLA_QK), GLA_RANK ** -0.5),
        'gla_b_up': nrm((N_EVEN, 2, GLA_QK), 0.1),
        'gla_norm_g': 1.0 + nrm((N_EVEN, GLA_V), 0.05),
        'rnn_conv_w': nrm((N_EVEN, RNN_CONV, RNN_WIDTH), RNN_CONV ** -0.5),
        'rnn_conv_b': nrm((N_EVEN, RNN_WIDTH), 0.01),
        'rnn_w_r': nrm((N_EVEN, 2, RNN_BLOCKS, RNN_BLOCK, RNN_BLOCK), RNN_BLOCK ** -0.5),
        'rnn_b_r': nrm((N_EVEN, 2, RNN_WIDTH), 0.01),
        'rnn_w_i': nrm((N_EVEN, 2, RNN_BLOCKS, RNN_BLOCK, RNN_BLOCK), RNN_BLOCK ** -0.5),
        'rnn_b_i': nrm((N_EVEN, 2, RNN_WIDTH), 0.01),
        'rnn_lam': rnn_lam,
        'od_w_in': nrm((N_ODD, D_MODEL, ODD_PROJ), D_MODEL ** -0.5),
        'od_w_out': nrm((N_ODD, SSD_INNER, D_MODEL), SSD_INNER ** -0.5),
        'ssd_conv_w': nrm((N_ODD, SSD_CONV, SSD_XBC), SSD_CONV ** -0.5),
        'ssd_conv_b': nrm((N_ODD, SSD_XBC), 0.01),
        'ssd_dt_bias': ssd_dt_bias,
        'ssd_a_log': jnp.log(unif((N_ODD, 2, SSD_HEADS), 1.0, 16.0)),
        'ssd_d': 1.0 + nrm((N_ODD, SSD_HEADS), 0.1),
        'ssd_norm_g': 1.0 + nrm((N_ODD, SSD_INNER), 0.05),
        'ffn_w_gate': nrm((DEPTH, D_MODEL, D_FF), D_MODEL ** -0.5),
        'ffn_w_up': nrm((DEPTH, D_MODEL, D_FF), D_MODEL ** -0.5),
        'ffn_w_down': nrm((DEPTH, D_FF, D_MODEL), D_FF ** -0.5),
    }


def reference(x_prompt, x_sample, state_gla, state_rglru, state_ssd, c, c_ctx,
              w_ada, b_ada, norm_g, ev_w_in, ev_w_out, gla_w_up, gla_b_up, gla_norm_g,
              rnn_conv_w, rnn_conv_b, rnn_w_r, rnn_b_r, rnn_w_i, rnn_b_i, rnn_lam,
              od_w_in, od_w_out, ssd_conv_w, ssd_conv_b, ssd_dt_bias, ssd_a_log, ssd_d, ssd_norm_g,
              ffn_w_gate, ffn_w_up, ffn_w_down):
    rows = x_sample.shape[1] // GRID_W
    bp = x_prompt.shape[0]
    f32 = jnp.float32
    zeros_gla = jnp.zeros((bp, 2, GLA_HEADS, GLA_DK, GLA_DV), f32)
    zeros_rnn = jnp.zeros((bp, 2, RNN_WIDTH), f32)
    zeros_ssd = jnp.zeros((bp, 2, SSD_HEADS, SSD_HEAD_DIM, SSD_STATE), f32)
    yp, ys = x_prompt, x_sample
    new_gla, new_rnn, new_ssd = [], [], []
    for l in range(DEPTH):
        mod_p = adaln(c_ctx[None], w_ada[l], b_ada[l])
        mod_s = adaln(c, w_ada[l], b_ada[l])
        hp = modulated_input(yp, mod_p, 0, norm_g[l, 0])
        hs = modulated_input(ys, mod_s, 0, norm_g[l, 0])
        if l % 2 == 0:
            e = l // 2
            ev = (ev_w_in[e], ev_w_out[e], gla_w_up[e], gla_b_up[e], gla_norm_g[e], rnn_conv_w[e], rnn_conv_b[e],
                  rnn_w_r[e], rnn_b_r[e], rnn_w_i[e], rnn_b_i[e], rnn_lam[e])
            out_p, sg, sr = even_mixer(hp, *ev, zeros_gla, zeros_rnn, None)
            out_s, _, _ = even_mixer(hs, *ev, state_gla[:, e], state_rglru[:, e], rows)
            new_gla.append(sg)
            new_rnn.append(sr)
        else:
            o = l // 2
            od = (od_w_in[o], od_w_out[o], ssd_conv_w[o], ssd_conv_b[o], ssd_dt_bias[o], ssd_a_log[o], ssd_d[o],
                  ssd_norm_g[o])
            out_p, ss = odd_mixer(hp, *od, zeros_ssd)
            out_s, _ = odd_mixer(hs, *od, state_ssd[:, o])
            new_ssd.append(ss)
        yp = gated_residual(yp, out_p, mod_p, 0, norm_g[l, 1])
        ys = gated_residual(ys, out_s, mod_s, 0, norm_g[l, 1])
        fp = swiglu(modulated_input(yp, mod_p, 1, norm_g[l, 2]), ffn_w_gate[l], ffn_w_up[l], ffn_w_down[l])
        fs = swiglu(modulated_input(ys, mod_s, 1, norm_g[l, 2]), ffn_w_gate[l], ffn_w_up[l], ffn_w_down[l])
        yp = gated_residual(yp, fp, mod_p, 1, norm_g[l, 3])
        ys = gated_residual(ys, fs, mod_s, 1, norm_g[l, 3])
    new_state_gla = jnp.stack(new_gla, axis=1)
    new_state_rglru = jnp.stack(new_rnn, axis=1)
    new_state_ssd = jnp.stack(new_ssd, axis=1)
    return (yp, ys, new_state_gla, new_state_rglru, new_state_ssd)
```

```python
import functools
import math

import jax
import jax.numpy as jnp
from jax import lax
from jax.experimental import pallas as pl
from jax.experimental.pallas import tpu as pltpu

F32 = jnp.float32
BF16 = jnp.bfloat16

EPS = 1e-6
MOD_COUNT = 6
GLA_HEADS = 4
GLA_RANK = 16
GLA_TAU = 16.0
GLA_CHUNK = 64
ROPE_BASE = 10000.0
GRID_W = 64
RNN_BLOCK = 128
RNN_C = 8.0
SSD_HEAD_DIM = 64
SSD_STATE = 128
SSD_GROUPS = 8
SSD_CHUNK = 128

LANES = 128
V7X_VMEM_BYTES = 64 * 1024 * 1024
VMEM_LIMIT = V7X_VMEM_BYTES - 8 * 1024 * 1024


def _cparams(sem):
    return pltpu.CompilerParams(dimension_semantics=sem, vmem_limit_bytes=VMEM_LIMIT)


def _sigmoid(x):
    return 1.0 / (1.0 + jnp.exp(-x))


def _silu(x):
    return x * _sigmoid(x)


def _softplus(x):
    return jnp.maximum(x, 0.0) + jnp.log1p(jnp.exp(-jnp.abs(x)))


def _log_sigmoid(x):
    return jnp.minimum(x, 0.0) - jnp.log1p(jnp.exp(-jnp.abs(x)))


def _gelu_tanh(x):
    return 0.5 * x * (1.0 + jnp.tanh(math.sqrt(2.0 / math.pi) * (x + 0.044715 * (x * x * x))))


def _rms(x, gain):
    ms = jnp.mean(x * x, axis=-1, keepdims=True)
    return x * lax.rsqrt(ms + EPS) * gain


def _split3(x):
    hi = x.astype(BF16)
    r1 = x - hi.astype(F32)
    mid = r1.astype(BF16)
    lo = (r1 - mid.astype(F32)).astype(BF16)
    return hi, mid, lo


def _dot(a, b):
    return jnp.dot(a, b, preferred_element_type=F32)


def _dot_nt(a, b):
    return lax.dot_general(a, b, (((1,), (1,)), ((), ())), preferred_element_type=F32)


def _dot_tn(a, b):
    return lax.dot_general(a, b, (((0,), (0,)), ((), ())), preferred_element_type=F32)


def _dot_exact_rhs(a_bf16, x):
    hi, mid, lo = _split3(x)
    return _dot(a_bf16, hi) + _dot(a_bf16, mid) + _dot(a_bf16, lo)


def _dot_exact_lhs(x, b_bf16):
    hi, mid, lo = _split3(x)
    return _dot(hi, b_bf16) + _dot(mid, b_bf16) + _dot(lo, b_bf16)


def _adaln_kernel(c_ref, w_ref, b_ref, o_ref):
    s = _silu(c_ref[...]).astype(BF16)
    o_ref[...] = _dot(s, w_ref[...].astype(BF16)) + b_ref[...]


def _adaln(cvec, w_ada, b_ada, tn=1024):
    depth, d, n = w_ada.shape
    r = cvec.shape[0]
    return pl.pallas_call(
        _adaln_kernel,
        grid=(depth, n // tn),
        in_specs=[
            pl.BlockSpec((r, d), lambda l, j: (0, 0)),
            pl.BlockSpec((None, d, tn), lambda l, j: (l, 0, j)),
            pl.BlockSpec((None, 1, tn), lambda l, j: (l, 0, j)),
        ],
        out_specs=pl.BlockSpec((None, r, tn), lambda l, j: (l, 0, j)),
        out_shape=jax.ShapeDtypeStruct((depth, r, n), F32),
        compiler_params=_cparams(("arbitrary", "arbitrary")),
        name="adaln",
    )(cvec, w_ada, b_ada.reshape(depth, 1, n))


def _modnorm(x, mod_ref, gain, j):
    shift = mod_ref[3 * j:3 * j + 1, :]
    scale = mod_ref[3 * j + 1:3 * j + 2, :]
    return _rms(x, gain) * (1.0 + scale) + shift


def _mm_in_kernel(x_ref, mod_ref, g_ref, w_ref, o_ref, h_ref):
    @pl.when(pl.program_id(1) == 0)
    def _():
        h_ref[...] = _modnorm(x_ref[...], mod_ref, g_ref[...], 0).astype(BF16)

    o_ref[...] = _dot(h_ref[...], w_ref[...]).astype(o_ref.dtype)


def _mm_in(x, mod, gain, w, tm=1024, tn=1152, out_dtype=F32):
    b, l, d = x.shape
    n = w.shape[1]
    m = b * l
    nb = mod.shape[0]
    tm = min(tm, l if nb > 1 else m)
    tiles_per_batch = l // tm
    mod_idx = (lambda i, j: (i // tiles_per_batch, 0, 0)) if nb > 1 else (lambda i, j: (0, 0, 0))
    out = pl.pallas_call(
        _mm_in_kernel,
        grid=(m // tm, n // tn),
        in_specs=[
            pl.BlockSpec((tm, d), lambda i, j: (i, 0)),
            pl.BlockSpec((None, MOD_COUNT, d), mod_idx),
            pl.BlockSpec((1, d), lambda i, j: (0, 0)),
            pl.BlockSpec((d, tn), lambda i, j: (0, j)),
        ],
        out_specs=pl.BlockSpec((tm, tn), lambda i, j: (i, j)),
        out_shape=jax.ShapeDtypeStruct((m, n), out_dtype),
        scratch_shapes=[pltpu.VMEM((tm, d), BF16)],
        compiler_params=_cparams(("arbitrary", "arbitrary")),
        name="mm_in",
    )(x.reshape(m, d), mod, gain, w)
    return out.reshape(b, l, n)


def _mm_out_kernel(*refs, n_lhs):
    lhs_refs = refs[:n_lhs]
    w_ref, x_ref, mod_ref, g_ref, o_ref = refs[n_lhs:]
    acc = None
    k0 = 0
    for a_ref in lhs_refs:
        kk = a_ref.shape[1]
        part = _dot(a_ref[...].astype(BF16), w_ref[k0:k0 + kk, :])
        acc = part if acc is None else acc + part
        k0 += kk
    gate = mod_ref[2:3, :]
    o_ref[...] = x_ref[...] + gate * _rms(acc, g_ref[...])


def _mm_out(lhs_list, w, x, mod, gain, tm=512):
    b, l, d = x.shape
    m = b * l
    nb = mod.shape[0]
    tm = min(tm, l if nb > 1 else m)
    tiles_per_batch = l // tm
    mod_idx = (lambda i: (i // tiles_per_batch, 0, 0)) if nb > 1 else (lambda i: (0, 0, 0))
    ktot = w.shape[0]
    in_specs = [pl.BlockSpec((tm, a.shape[-1]), lambda i: (i, 0)) for a in lhs_list]
    in_specs += [
        pl.BlockSpec((ktot, d), lambda i: (0, 0), pipeline_mode=pl.Buffered(1)),
        pl.BlockSpec((tm, d), lambda i: (i, 0)),
        pl.BlockSpec((None, MOD_COUNT, d), mod_idx),
        pl.BlockSpec((1, d), lambda i: (0, 0)),
    ]
    out = pl.pallas_call(
        functools.partial(_mm_out_kernel, n_lhs=len(lhs_list)),
        grid=(m // tm,),
        in_specs=in_specs,
        out_specs=pl.BlockSpec((tm, d), lambda i: (i, 0)),
        out_shape=jax.ShapeDtypeStruct((m, d), F32),
        compiler_params=_cparams(("arbitrary",)),
        name="mm_out",
    )(*[a.reshape(m, a.shape[-1]) for a in lhs_list], w, x.reshape(m, d), mod, gain)
    return out.reshape(b, l, d)


def _ffn_kernel(x_ref, mod_ref, g_ref, wg_ref, wu_ref, wd_ref, o_ref, h_ref, acc_ref):
    f = pl.program_id(1)

    @pl.when(f == 0)
    def _():
        h_ref[...] = _modnorm(x_ref[...], mod_ref, g_ref[0:1, :], 1).astype(BF16)
        acc_ref[...] = jnp.zeros_like(acc_ref)

    h = h_ref[...]
    a = _silu(_dot(h, wg_ref[...])) * _dot(h, wu_ref[...])
    acc_ref[...] += _dot(a.astype(BF16), wd_ref[...])

    @pl.when(f == pl.num_programs(1) - 1)
    def _():
        gate = mod_ref[5:6, :]
        o_ref[...] = x_ref[...] + gate * _rms(acc_ref[...], g_ref[1:2, :])


def _ffn(x, mod, gains, wg, wu, wd, tm=512, tf=512):
    b, l, d = x.shape
    m = b * l
    ff = wg.shape[1]
    nb = mod.shape[0]
    tm = min(tm, l if nb > 1 else m)
    tiles_per_batch = l // tm
    mod_idx = (lambda i, f: (i // tiles_per_batch, 0, 0)) if nb > 1 else (lambda i, f: (0, 0, 0))
    out = pl.pallas_call(
        _ffn_kernel,
        grid=(m // tm, ff // tf),
        in_specs=[
            pl.BlockSpec((tm, d), lambda i, f: (i, 0)),
            pl.BlockSpec((None, MOD_COUNT, d), mod_idx),
            pl.BlockSpec((2, d), lambda i, f: (0, 0)),
            pl.BlockSpec((d, tf), lambda i, f: (0, f)),
            pl.BlockSpec((d, tf), lambda i, f: (0, f)),
            pl.BlockSpec((tf, d), lambda i, f: (f, 0)),
        ],
        out_specs=pl.BlockSpec((tm, d), lambda i, f: (i, 0)),
        out_shape=jax.ShapeDtypeStruct((m, d), F32),
        scratch_shapes=[pltpu.VMEM((tm, d), BF16), pltpu.VMEM((tm, d), F32)],
        compiler_params=_cparams(("arbitrary", "arbitrary")),
        name="ffn",
    )(x.reshape(m, d), mod, gains, wg, wu, wd)
    return out.reshape(b, l, d)


def _rope_rotate(x, cos, sin):
    half = x.shape[1] // 2
    swapped = jnp.concatenate(
        [pltpu.roll(x[:, :half], half // 2, axis=1), pltpu.roll(x[:, half:], half // 2, axis=1)], axis=1)
    return x * cos + swapped * sin


def _gla_kernel(*refs, seq, rope, has_s0, want_state):
    it = iter(refs)
    q_ref, k_ref, v_ref, g_ref, lr_ref, wup_ref, wupt_ref, bup_ref, bupt_ref, gn_ref = [next(it) for _ in range(10)]
    cos_ref = sin_ref = s0_ref = snew_ref = None
    if rope:
        cos_ref, sin_ref = next(it), next(it)
    if has_s0:
        s0_ref = next(it)
    o_ref = next(it)
    if want_state:
        snew_ref = next(it)
    s_ref, of_ref = next(it), next(it)

    cs = GLA_CHUNK
    nc = seq // cs
    dk = q_ref.shape[1]
    ri = lax.broadcasted_iota(jnp.int32, (cs, cs), 0)
    ci = lax.broadcasted_iota(jnp.int32, (cs, cs), 1)
    masks = (ri >= ci, ri <= ci)
    masks_bf = tuple(jnp.where(m, 1.0, 0.0).astype(BF16) for m in masks)

    def chunk(c, dirn):
        rows = pl.ds(pl.multiple_of(c * cs, cs), cs)
        q = q_ref[rows, :] * (dk ** -0.5)
        k = k_ref[rows, :]
        if rope:
            cos, sin = cos_ref[rows, :], sin_ref[rows, :]
            q = _rope_rotate(q, cos, sin)
            k = _rope_rotate(k, cos, sin)
        v = v_ref[rows, :].astype(BF16)
        lr = lr_ref[rows, :].astype(BF16)
        la = _log_sigmoid(_dot(lr, wup_ref[dirn]) + bup_ref[dirn]) * (1.0 / GLA_TAU)
        la_t = _log_sigmoid(_dot_nt(wupt_ref[dirn], lr) + bupt_ref[dirn]) * (1.0 / GLA_TAU)
        b_all_col = jnp.sum(la_t, axis=1, keepdims=True)
        b = _dot_exact_rhs(masks_bf[dirn], la)
        b_all_row = b[cs - 1:cs, :] if dirn == 0 else b[0:1, :]
        qd = (q * jnp.exp(b)).astype(BF16)
        kd = (k * jnp.exp(-b)).astype(BF16)
        sc = jnp.where(masks[dirn], _dot_nt(qd, kd), 0.0)
        s_old = s_ref[...]
        o = _dot(sc.astype(BF16), v) + _dot(qd, s_old.astype(BF16))
        k2 = (k * jnp.exp(b_all_row - b)).astype(BF16)
        s_ref[...] = jnp.exp(b_all_col) * s_old + _dot_tn(k2, v)
        return o

    def init_state(dirn):
        if has_s0:
            s_ref[...] = s0_ref[dirn]
        else:
            s_ref[...] = jnp.zeros_like(s_ref)

    init_state(0)

    def fwd(c, carry):
        rows = pl.ds(pl.multiple_of(c * cs, cs), cs)
        of_ref[rows, :] = chunk(c, 0)
        return carry

    lax.fori_loop(0, nc, fwd, 0)
    if want_state:
        snew_ref[0] = s_ref[...]
    init_state(1)

    def bwd(i, carry):
        c = nc - 1 - i
        rows = pl.ds(pl.multiple_of(c * cs, cs), cs)
        o = of_ref[rows, :] + chunk(c, 1)
        o = _rms(o, gn_ref[...]) * _silu(g_ref[rows, :])
        o_ref[rows, :] = o.astype(o_ref.dtype)
        return carry

    lax.fori_loop(0, nc, bwd, 0)
    if want_state:
        snew_ref[1] = s_ref[...]


def _gla(proj, col, wup, wupt, bup, bupt, gn, rope_tabs, s0, want_state):
    b, l, _ = proj.shape
    h = GLA_HEADS
    dk = wup.shape[2] // h
    dv = gn.shape[1] // h
    rope = rope_tabs is not None
    has_s0 = s0 is not None
    in_specs = [
        pl.BlockSpec((None, l, dk), lambda bi, hi: (bi, 0, col["q"] // dk + hi)),
        pl.BlockSpec((None, l, dk), lambda bi, hi: (bi, 0, col["k"] // dk + hi)),
        pl.BlockSpec((None, l, dv), lambda bi, hi: (bi, 0, col["v"] // dv + hi)),
        pl.BlockSpec((None, l, dv), lambda bi, hi: (bi, 0, col["g"] // dv + hi)),
        pl.BlockSpec((None, l, LANES), lambda bi, hi: (bi, 0, col["lr"] // LANES)),
        pl.BlockSpec((2, LANES, dk), lambda bi, hi: (0, 0, hi)),
        pl.BlockSpec((2, dk, LANES), lambda bi, hi: (0, hi, 0)),
        pl.BlockSpec((2, 1, dk), lambda bi, hi: (0, 0, hi)),
        pl.BlockSpec((2, dk, 1), lambda bi, hi: (0, hi, 0)),
        pl.BlockSpec((1, dv), lambda bi, hi: (0, hi)),
    ]
    args = [proj, proj, proj, proj, proj, wup, wupt, bup, bupt, gn]
    if rope:
        in_specs += [pl.BlockSpec((l, dk), lambda bi, hi: (0, 0), pipeline_mode=pl.Buffered(1))] * 2
        args += list(rope_tabs)
    if has_s0:
        in_specs.append(pl.BlockSpec((None, 2, None, dk, dv), lambda bi, hi: (bi, 0, hi, 0, 0)))
        args.append(s0)
    out_specs = [pl.BlockSpec((None, l, dv), lambda bi, hi: (bi, 0, hi))]
    out_shape = [jax.ShapeDtypeStruct((b, l, h * dv), BF16)]
    if want_state:
        out_specs.append(pl.BlockSpec((None, 2, None, dk, dv), lambda bi, hi: (bi, 0, hi, 0, 0)))
        out_shape.append(jax.ShapeDtypeStruct((b, 2, h, dk, dv), F32))
    res = pl.pallas_call(
        functools.partial(_gla_kernel, seq=l, rope=rope, has_s0=has_s0, want_state=want_state),
        grid=(b, h),
        in_specs=in_specs,
        out_specs=out_specs,
        out_shape=out_shape,
        scratch_shapes=[pltpu.VMEM((dk, dv), F32), pltpu.VMEM((l, dv), F32)],
        compiler_params=_cparams(("arbitrary", "arbitrary")),
        name="gla",
    )(*args)
    return res if want_state else res[0]


def _tile_scan(a, u, reverse):
    n = a.shape[0]
    rid = lax.broadcasted_iota(jnp.int32, a.shape, 0)
    for d in (1, 2, 4):
        shift, m = (n - d, rid < n - d) if reverse else (d, rid >= d)
        a_s = pltpu.roll(a, shift, axis=0)
        u_s = pltpu.roll(u, shift, axis=0)
        u = jnp.where(m, u + a * u_s, u)
        a = jnp.where(m, a * a_s, a)
    return a, u


def _rglru_kernel(*refs, seq, has_h0, want_state, tc):
    it = iter(refs)
    xr_ref, yr_ref, cw_ref, cb_ref, wg_ref, bg_ref, lam_ref = [next(it) for _ in range(7)]
    h0_ref = next(it) if has_h0 else None
    y_ref = next(it)
    hl_ref = next(it) if want_state else None
    xpad_ref, af_ref, uf_ref, ab_ref, ub_ref = [next(it) for _ in range(5)]

    wb = xr_ref.shape[1]
    nblk = wb // RNN_BLOCK
    halo = 8
    xpad_ref[0:halo, :] = jnp.zeros((halo, wb), F32)
    xpad_ref[seq + halo:seq + 2 * halo, :] = jnp.zeros((halo, wb), F32)
    xpad_ref[halo:seq + halo, :] = xr_ref[...]
    ls = _log_sigmoid(lam_ref[...])

    def gates(c, carry):
        r0 = pl.multiple_of(c * tc, tc)
        xw = xpad_ref[pl.ds(r0, tc + 2 * halo), :]
        n = tc + 2 * halo
        xc = (cw_ref[0:1, :] * pltpu.roll(xw, 2, axis=0)[halo:halo + tc]
              + cw_ref[1:2, :] * pltpu.roll(xw, 1, axis=0)[halo:halo + tc]
              + cw_ref[2:3, :] * xw[halo:halo + tc]
              + cw_ref[3:4, :] * pltpu.roll(xw, n - 1, axis=0)[halo:halo + tc]
              + cb_ref[...])
        rows = pl.ds(r0, tc)
        for j in range(nblk):
            lanes = slice(j * RNN_BLOCK, (j + 1) * RNN_BLOCK)
            xcj = xc[:, lanes]
            gt = _dot(xcj.astype(BF16), wg_ref[j]) + bg_ref[j]
            for dirn, (a_ref, u_ref) in enumerate(((af_ref, uf_ref), (ab_ref, ub_ref))):
                r = _sigmoid(gt[:, dirn * RNN_BLOCK:(dirn + 1) * RNN_BLOCK])
                i = _sigmoid(gt[:, (2 + dirn) * RNN_BLOCK:(3 + dirn) * RNN_BLOCK])
                log_a = RNN_C * r * ls[dirn:dirn + 1, lanes]
                a = jnp.exp(log_a)
                u = jnp.sqrt(-jnp.tanh(log_a) * (a * a + 1.0)) * i * xcj
                a_ref[rows, lanes] = a
                u_ref[rows, lanes] = u
        return carry

    lax.fori_loop(0, seq // tc, gates, 0)

    ntile = seq // 8
    h0f = h0_ref[0:1, :] if has_h0 else jnp.zeros((1, wb), F32)
    h0b = h0_ref[1:2, :] if has_h0 else jnp.zeros((1, wb), F32)

    def fwd(t, hc):
        rows = pl.ds(pl.multiple_of(t * 8, 8), 8)
        a, u = _tile_scan(af_ref[rows, :], uf_ref[rows, :], False)
        hh = u + a * hc
        uf_ref[rows, :] = hh
        return hh[7:8, :]

    hlf = lax.fori_loop(0, ntile, fwd, h0f, unroll=4)

    def bwd(i, hc):
        t = ntile - 1 - i
        rows = pl.ds(pl.multiple_of(t * 8, 8), 8)
        a, u = _tile_scan(ab_ref[rows, :], ub_ref[rows, :], True)
        hh = u + a * hc
        y = (uf_ref[rows, :] + hh) * _gelu_tanh(yr_ref[rows, :])
        y_ref[rows, :] = y.astype(y_ref.dtype)
        return hh[0:1, :]

    hlb = lax.fori_loop(0, ntile, bwd, h0b, unroll=4)
    if want_state:
        hl_ref[0:1, :] = hlf
        hl_ref[1:2, :] = hlb


def _rglru(proj, col, cw, cb, wg, bg, lam, h0, want_state, wb=256, tc=256):
    b, l, _ = proj.shape
    w = cw.shape[1]
    tc = min(tc, l)
    nblk = wb // RNN_BLOCK
    has_h0 = h0 is not None
    in_specs = [
        pl.BlockSpec((None, l, wb), lambda bi, j: (bi, 0, col["xr"] // wb + j)),
        pl.BlockSpec((None, l, wb), lambda bi, j: (bi, 0, col["yr"] // wb + j)),
        pl.BlockSpec((4, wb), lambda bi, j: (0, j)),
        pl.BlockSpec((1, wb), lambda bi, j: (0, j)),
        pl.BlockSpec((nblk, RNN_BLOCK, 4 * RNN_BLOCK), lambda bi, j: (j, 0, 0)),
        pl.BlockSpec((nblk, 1, 4 * RNN_BLOCK), lambda bi, j: (j, 0, 0)),
        pl.BlockSpec((2, wb), lambda bi, j: (0, j)),
    ]
    args = [proj, proj, cw, cb, wg, bg, lam]
    if has_h0:
        in_specs.append(pl.BlockSpec((None, 2, wb), lambda bi, j: (bi, 0, j)))
        args.append(h0)
    out_specs = [pl.BlockSpec((None, l, wb), lambda bi, j: (bi, 0, j))]
    out_shape = [jax.ShapeDtypeStruct((b, l, w), BF16)]
    if want_state:
        out_specs.append(pl.BlockSpec((None, 2, wb), lambda bi, j: (bi, 0, j)))
        out_shape.append(jax.ShapeDtypeStruct((b, 2, w), F32))
    res = pl.pallas_call(
        functools.partial(_rglru_kernel, seq=l, has_h0=has_h0, want_state=want_state, tc=tc),
        grid=(b, w // wb),
        in_specs=in_specs,
        out_specs=out_specs,
        out_shape=out_shape,
        scratch_shapes=[pltpu.VMEM((l + 16, wb), F32)] + [pltpu.VMEM((l, wb), F32)] * 4,
        compiler_params=_cparams(("arbitrary", "arbitrary")),
        name="rglru",
    )(*args)
    return res if want_state else res[0]


def _tri_masks(n):
    ri = lax.broadcasted_iota(jnp.int32, (n, n), 0)
    ci = lax.broadcasted_iota(jnp.int32, (n, n), 1)
    return ri >= ci, ri <= ci


def _ssd_dt_kernel(dt_ref, bias_ref, alog_ref, dtr_ref, cfr_ref, cbr_ref, *, seq):
    cs = SSD_CHUNK
    tril, triu = _tri_masks(cs)
    tril_bf = jnp.where(tril, 1.0, 0.0).astype(BF16)
    triu_bf = jnp.where(triu, 1.0, 0.0).astype(BF16)
    a = -jnp.exp(alog_ref[...])

    def body(c, carry):
        rows = pl.ds(pl.multiple_of(c * cs, cs), cs)
        dt = _softplus(dt_ref[rows, :] + bias_ref[...])
        la = dt * a
        dtr_ref[c] = dt.T
        cfr_ref[c] = _dot_exact_rhs(tril_bf, la).T
        cbr_ref[c] = _dot_exact_rhs(triu_bf, la).T
        return carry

    lax.fori_loop(0, seq // cs, body, 0)


def _ssd_dt(proj, col, bias, alog):
    b, l, _ = proj.shape
    nc = l // SSD_CHUNK
    shp = jax.ShapeDtypeStruct((b, nc, LANES, SSD_CHUNK), F32)
    ospec = pl.BlockSpec((None, nc, LANES, SSD_CHUNK), lambda bi: (bi, 0, 0, 0))
    return pl.pallas_call(
        functools.partial(_ssd_dt_kernel, seq=l),
        grid=(b,),
        in_specs=[
            pl.BlockSpec((None, l, LANES), lambda bi: (bi, 0, col["dt"] // LANES)),
            pl.BlockSpec((1, LANES), lambda bi: (0, 0)),
            pl.BlockSpec((1, LANES), lambda bi: (0, 0)),
        ],
        out_specs=[ospec, ospec, ospec],
        out_shape=[shp, shp, shp],
        compiler_params=_cparams(("arbitrary",)),
        name="ssd_dt",
    )(proj, bias, alog)


def _conv4_silu(pad_ref, r0, n, halo, w_ref, b_ref):
    xw = pad_ref[pl.ds(r0, n + 2 * halo), :]
    tot = n + 2 * halo
    xc = (w_ref[0:1, :] * pltpu.roll(xw, 2, axis=0)[halo:halo + n]
          + w_ref[1:2, :] * pltpu.roll(xw, 1, axis=0)[halo:halo + n]
          + w_ref[2:3, :] * xw[halo:halo + n]
          + w_ref[3:4, :] * pltpu.roll(xw, tot - 1, axis=0)[halo:halo + n]
          + b_ref[...])
    return _silu(xc)


def _ssd_kernel(*refs, seq, has_s0, want_state):
    it = iter(refs)
    (z_ref, x_ref, bm_ref, cm_ref, dtr_ref, cfr_ref, cbr_ref, cwx_ref, cwb_ref, cwc_ref,
     cbx_ref, cbb_ref, cbc_ref, dsk_ref, gn_ref) = [next(it) for _ in range(15)]
    s0_ref = next(it) if has_s0 else None
    y_ref = next(it)
    snew_ref = next(it) if want_state else None
    xpad_ref, bpad_ref, cpad_ref, xs_ref, bs_ref, cs_ref, yf_ref, s_ref = [next(it) for _ in range(8)]

    cs = SSD_CHUNK
    nc = seq // cs
    hp = SSD_HEAD_DIM
    gw = x_ref.shape[1]
    hg = gw // hp
    halo = 8
    for pad_ref, src_ref in ((xpad_ref, x_ref), (bpad_ref, bm_ref), (cpad_ref, cm_ref)):
        wd = pad_ref.shape[1]
        pad_ref[0:halo, :] = jnp.zeros((halo, wd), F32)
        pad_ref[seq + halo:seq + 2 * halo, :] = jnp.zeros((halo, wd), F32)
        pad_ref[halo:seq + halo, :] = src_ref[...]

    def conv(c, carry):
        r0 = pl.multiple_of(c * cs, cs)
        rows = pl.ds(r0, cs)
        xs_ref[rows, :] = _conv4_silu(xpad_ref, r0, cs, halo, cwx_ref, cbx_ref)
        bs_ref[rows, :] = _conv4_silu(bpad_ref, r0, cs, halo, cwb_ref, cbb_ref)
        cs_ref[rows, :] = _conv4_silu(cpad_ref, r0, cs, halo, cwc_ref, cbc_ref)
        return carry

    lax.fori_loop(0, nc, conv, 0)

    masks = _tri_masks(cs)
    er = lax.broadcasted_iota(jnp.int32, (2 * hg, gw), 0)
    el = lax.broadcasted_iota(jnp.int32, (2 * hg, gw), 1) // hp
    etr = lax.broadcasted_iota(jnp.int32, (gw, 2 * hg), 0) // hp
    etc = lax.broadcasted_iota(jnp.int32, (gw, 2 * hg), 1)
    expand = tuple(jnp.where(er == el + d * hg, 1.0, 0.0).astype(BF16) for d in (0, 1))
    expand_t = tuple(jnp.where(etc == etr + d * hg, 1.0, 0.0).astype(BF16) for d in (0, 1))
    lane_in_pair = lax.broadcasted_iota(jnp.int32, (cs, 2 * hp), 1) // hp

    def chunk(c, dirn):
        rows = pl.ds(pl.multiple_of(c * cs, cs), cs)
        xs = xs_ref[rows, :]
        b_c = bs_ref[rows, :].astype(BF16)
        c_c = cs_ref[rows, :].astype(BF16)
        cum_r = (cfr_ref if dirn == 0 else cbr_ref)[c]
        cum_c = cum_r.T
        dt_c = dtr_ref[c].T
        last = cs - 1 if dirn == 0 else 0
        cum_end_row = cum_c[last:last + 1, :]
        stack = jnp.concatenate([dt_c, jnp.exp(cum_c), jnp.exp(cum_end_row - cum_c)], axis=0)
        ex = _dot_exact_lhs(stack, expand[dirn])
        xdt = xs * ex[0:cs]
        ecum = ex[cs:2 * cs]
        to_end = ex[2 * cs:3 * cs]
        cb = _dot_nt(c_c, b_c)
        tiles = []
        for m in range(hg // 2):
            xpair = xdt[:, m * 2 * hp:(m + 1) * 2 * hp]
            acc = None
            for jj in range(2):
                hcol = dirn * hg + 2 * m + jj
                seg = cum_c[:, hcol:hcol + 1] - cum_r[hcol:hcol + 1, :]
                dec = jnp.exp(jnp.where(masks[dirn], seg, -jnp.inf))
                xj = jnp.where(lane_in_pair == jj, xpair, 0.0).astype(BF16)
                part = _dot((cb * dec).astype(BF16), xj)
                acc = part if acc is None else acc + part
            tiles.append(acc)
        s_old = s_ref[...]
        y = jnp.concatenate(tiles, axis=1) + _dot_nt(c_c, s_old.astype(BF16)) * ecum
        end_col = jnp.broadcast_to(cum_r[:, last:last + 1], (2 * hg, LANES))
        scale = jnp.exp(_dot_exact_rhs(expand_t[dirn], end_col))
        s_ref[...] = scale * s_old + _dot_tn((xdt * to_end).astype(BF16), b_c)
        return y

    def init_state(dirn):
        if has_s0:
            s_ref[...] = s0_ref[dirn]
        else:
            s_ref[...] = jnp.zeros_like(s_ref)

    init_state(0)

    def fwd(c, carry):
        rows = pl.ds(pl.multiple_of(c * cs, cs), cs)
        yf_ref[rows, :] = chunk(c, 0)
        return carry

    lax.fori_loop(0, nc, fwd, 0)
    if want_state:
        snew_ref[0] = s_ref[...]
    init_state(1)

    def bwd(i, carry):
        c = nc - 1 - i
        rows = pl.ds(pl.multiple_of(c * cs, cs), cs)
        y = yf_ref[rows, :] + chunk(c, 1) + dsk_ref[...] * xs_ref[rows, :]
        y = y * _silu(z_ref[rows, :])
        y_ref[rows, :] = _rms(y, gn_ref[...]).astype(y_ref.dtype)
        return carry

    lax.fori_loop(0, nc, bwd, 0)
    if want_state:
        snew_ref[1] = s_ref[...]


def _ssd(proj, col, dtr, cfr, cbr, cw, cb, dsk, gn, s0, want_state):
    b, l, _ = proj.shape
    g = SSD_GROUPS
    inner = dsk.shape[1]
    gw = inner // g
    ns = SSD_STATE
    nc = l // SSD_CHUNK
    hg2 = 2 * gw // SSD_HEAD_DIM
    has_s0 = s0 is not None
    bcol0 = inner // ns
    ccol0 = bcol0 + g
    rowspec = pl.BlockSpec((None, nc, hg2, SSD_CHUNK), lambda bi, gi: (bi, 0, gi, 0))
    in_specs = [
        pl.BlockSpec((None, l, gw), lambda bi, gi: (bi, 0, col["z"] // gw + gi)),
        pl.BlockSpec((None, l, gw), lambda bi, gi: (bi, 0, col["x"] // gw + gi)),
        pl.BlockSpec((None, l, ns), lambda bi, gi: (bi, 0, col["B"] // ns + gi)),
        pl.BlockSpec((None, l, ns), lambda bi, gi: (bi, 0, col["C"] // ns + gi)),
        rowspec, rowspec, rowspec,
        pl.BlockSpec((4, gw), lambda bi, gi: (0, gi)),
        pl.BlockSpec((4, ns), lambda bi, gi: (0, bcol0 + gi)),
        pl.BlockSpec((4, ns), lambda bi, gi: (0, ccol0 + gi)),
        pl.BlockSpec((1, gw), lambda bi, gi: (0, gi)),
        pl.BlockSpec((1, ns), lambda bi, gi: (0, bcol0 + gi)),
        pl.BlockSpec((1, ns), lambda bi, gi: (0, ccol0 + gi)),
        pl.BlockSpec((1, gw), lambda bi, gi: (0, gi)),
        pl.BlockSpec((1, gw), lambda bi, gi: (0, gi)),
    ]
    args = [proj, proj, proj, proj, dtr, cfr, cbr, cw, cw, cw, cb, cb, cb, dsk, gn]
    if has_s0:
        in_specs.append(pl.BlockSpec((None, 2, gw, ns), lambda bi, gi: (bi, 0, gi, 0)))
        args.append(s0)
    out_specs = [pl.BlockSpec((None, l, gw), lambda bi, gi: (bi, 0, gi))]
    out_shape = [jax.ShapeDtypeStruct((b, l, inner), BF16)]
    if want_state:
        out_specs.append(pl.BlockSpec((None, 2, gw, ns), lambda bi, gi: (bi, 0, gi, 0)))
        out_shape.append(jax.ShapeDtypeStruct((b, 2, inner, ns), F32))
    res = pl.pallas_call(
        functools.partial(_ssd_kernel, seq=l, has_s0=has_s0, want_state=want_state),
        grid=(b, g),
        in_specs=in_specs,
        out_specs=out_specs,
        out_shape=out_shape,
        scratch_shapes=[
            pltpu.VMEM((l + 16, gw), F32), pltpu.VMEM((l + 16, ns), F32), pltpu.VMEM((l + 16, ns), F32),
            pltpu.VMEM((l, gw), F32), pltpu.VMEM((l, ns), F32), pltpu.VMEM((l, ns), F32),
            pltpu.VMEM((l, gw), F32), pltpu.VMEM((gw, ns), F32),
        ],
        compiler_params=_cparams(("arbitrary", "arbitrary")),
        name="ssd",
    )(*args)
    return res if want_state else res[0]


def _prep_even(w_in, w_out, w_up, b_up, w_r, b_r, w_i, b_i):
    d = w_in.shape[0]
    h = GLA_HEADS
    qk = w_up.shape[2]
    gv = (w_in.shape[1] - 2 * qk - 2 * GLA_RANK) // 4
    sizes = (qk, qk, gv, gv, 2 * GLA_RANK, gv, gv)
    offs = [0]
    for s in sizes:
        offs.append(offs[-1] + s)
    q, k, v, g, lr, xr, yr = (w_in[:, offs[i]:offs[i + 1]] for i in range(7))
    pad = jnp.zeros((d, LANES - 2 * GLA_RANK), w_in.dtype)
    w = jnp.concatenate([q, k, v, g, xr, yr, lr, pad], axis=1).astype(BF16)
    col = {"q": 0, "k": qk, "v": 2 * qk, "g": 2 * qk + gv, "xr": 2 * qk + 2 * gv, "yr": 2 * qk + 3 * gv,
           "lr": 2 * qk + 4 * gv}
    wup = jnp.zeros((2, LANES, qk), F32)
    wup = wup.at[0, 0:GLA_RANK].set(w_up[0]).at[1, GLA_RANK:2 * GLA_RANK].set(w_up[1])
    wup = wup.astype(BF16)
    nblk = w_r.shape[1]
    wg = jnp.concatenate([w_r[0], w_r[1], w_i[0], w_i[1]], axis=-1).astype(BF16)
    bg = jnp.concatenate([b.reshape(nblk, 1, RNN_BLOCK) for b in (b_r[0], b_r[1], b_i[0], b_i[1])], axis=-1)
    return {
        "w_in": w, "col": col, "w_out": w_out.astype(BF16),
        "wup": wup, "wupt": jnp.swapaxes(wup, 1, 2),
        "bup": b_up.reshape(2, 1, qk), "bupt": b_up.reshape(2, qk, 1),
        "wg": wg, "bg": bg,
    }


def _group_major(a, heads):
    g = SSD_GROUPS
    hg = heads // g
    lead = a.shape[:-1]
    return jnp.swapaxes(a.reshape(lead + (2, g, hg)), -3, -2).reshape(lead + (2 * heads,))


def _prep_odd(w_in, w_out, dt_bias, a_log, d_skip):
    heads = d_skip.shape[0]
    inner = heads * SSD_HEAD_DIM
    xbc = inner + 2 * SSD_GROUPS * SSD_STATE
    dtw = _group_major(w_in[:, inner + xbc:], heads)
    w = jnp.concatenate([w_in[:, :inner + xbc], dtw], axis=1).astype(BF16)
    col = {"z": 0, "x": inner, "B": 2 * inner, "C": 2 * inner + SSD_GROUPS * SSD_STATE, "dt": inner + xbc}
    return {
        "w_in": w, "col": col, "w_out": w_out.astype(BF16),
        "dt_bias": _group_major(dt_bias.reshape(1, 2 * heads), heads),
        "a_log": _group_major(a_log.reshape(1, 2 * heads), heads),
        "dsk": jnp.repeat(d_skip, SSD_HEAD_DIM).reshape(1, inner),
    }


def _rope_tables(seq, dk):
    nf = dk // 4
    inv = ROPE_BASE ** (-jnp.arange(nf, dtype=F32) / nf)
    t = jnp.arange(seq)
    ang_r = (t // GRID_W).astype(F32)[:, None] * inv
    ang_c = (t % GRID_W).astype(F32)[:, None] * inv
    cos = jnp.concatenate([jnp.cos(ang_r)] * 2 + [jnp.cos(ang_c)] * 2, axis=1)
    sin = jnp.concatenate([-jnp.sin(ang_r), jnp.sin(ang_r), -jnp.sin(ang_c), jnp.sin(ang_c)], axis=1)
    return cos, sin


def _even_mixer(x, mod, gains, p, gla_gn, cw, cb, lam, s0_gla, h0_rnn, rope_tabs, want_state):
    proj = _mm_in(x, mod, gains[0:1], p["w_in"])
    gla = _gla(proj, p["col"], p["wup"], p["wupt"], p["bup"], p["bupt"], gla_gn, rope_tabs, s0_gla, want_state)
    rnn = _rglru(proj, p["col"], cw, cb, p["wg"], p["bg"], lam, h0_rnn, want_state)
    if want_state:
        (o, sg), (yr, sr) = gla, rnn
    else:
        (o, sg), (yr, sr) = (gla, None), (rnn, None)
    y = _mm_out([o, yr], p["w_out"], x, mod, gains[1:2])
    return y, sg, sr


def _odd_mixer(x, mod, gains, p, cw, cb, gn, s0, want_state):
    proj = _mm_in(x, mod, gains[0:1], p["w_in"])
    dtr, cfr, cbr = _ssd_dt(proj, p["col"], p["dt_bias"], p["a_log"])
    res = _ssd(proj, p["col"], dtr, cfr, cbr, cw, cb, p["dsk"], gn, s0, want_state)
    yv, ss = res if want_state else (res, None)
    y = _mm_out([yv], p["w_out"], x, mod, gains[1:2])
    return y, ss


def kernel(x_prompt, x_sample, state_gla, state_rglru, state_ssd, c, c_ctx, w_ada, b_ada, norm_g, ev_w_in, ev_w_out,
           gla_w_up, gla_b_up, gla_norm_g, rnn_conv_w, rnn_conv_b, rnn_w_r, rnn_b_r, rnn_w_i, rnn_b_i, rnn_lam,
           od_w_in, od_w_out, ssd_conv_w, ssd_conv_b, ssd_dt_bias, ssd_a_log, ssd_d, ssd_norm_g,
           ffn_w_gate, ffn_w_up, ffn_w_down):
    depth, d = norm_g.shape[0], norm_g.shape[2]
    nb = c.shape[0]
    rows = -(-(nb + 1) // 8) * 8
    cvec = jnp.zeros((rows, d), F32).at[0].set(c_ctx).at[1:nb + 1].set(c)
    mods = _adaln(cvec, w_ada, b_ada).reshape(depth, rows, MOD_COUNT, d)
    rope_tabs = _rope_tables(x_sample.shape[1], gla_w_up.shape[3] // GLA_HEADS)

    yp, ys = x_prompt, x_sample
    new_gla, new_rnn, new_ssd = [], [], []
    for l in range(depth):
        mod_p, mod_s = mods[l, 0:1], mods[l, 1:nb + 1]
        gains = norm_g[l]
        if l % 2 == 0:
            e = l // 2
            p = _prep_even(ev_w_in[e], ev_w_out[e], gla_w_up[e], gla_b_up[e], rnn_w_r[e], rnn_b_r[e],
                           rnn_w_i[e], rnn_b_i[e])
            gn = gla_norm_g[e].reshape(1, -1)
            cw, cb, lam = rnn_conv_w[e], rnn_conv_b[e].reshape(1, -1), rnn_lam[e]
            yp, sg, sr = _even_mixer(yp, mod_p, gains, p, gn, cw, cb, lam, None, None, None, True)
            ys, _, _ = _even_mixer(ys, mod_s, gains, p, gn, cw, cb, lam, state_gla[:, e], state_rglru[:, e],
                                   rope_tabs, False)
            new_gla.append(sg)
            new_rnn.append(sr)
        else:
            o = l // 2
            p = _prep_odd(od_w_in[o], od_w_out[o], ssd_dt_bias[o], ssd_a_log[o], ssd_d[o])
            cw, cb, gn = ssd_conv_w[o], ssd_conv_b[o].reshape(1, -1), ssd_norm_g[o].reshape(1, -1)
            inner, ns = gn.shape[1], SSD_STATE
            s0 = state_ssd[:, o].reshape(nb, 2, inner, ns)
            yp, ss = _odd_mixer(yp, mod_p, gains, p, cw, cb, gn, None, True)
            ys, _ = _odd_mixer(ys, mod_s, gains, p, cw, cb, gn, s0, False)
            new_ssd.append(ss.reshape(ss.shape[0], 2, inner // SSD_HEAD_DIM, SSD_HEAD_DIM, ns))
        wg, wu, wd = ffn_w_gate[l].astype(BF16), ffn_w_up[l].astype(BF16), ffn_w_down[l].astype(BF16)
        yp = _ffn(yp, mod_p, gains[2:4], wg, wu, wd)
        ys = _ffn(ys, mod_s, gains[2:4], wg, wu, wd)
    return (yp, ys, jnp.stack(new_gla, axis=1), jnp.stack(new_rnn, axis=1), jnp.stack(new_ssd, axis=1))
```

```python
import functools
import math

import jax
import jax.numpy as jnp
from jax import lax
from jax.experimental import pallas as pl
from jax.experimental.pallas import tpu as pltpu

F32 = jnp.float32
BF16 = jnp.bfloat16

EPS = 1e-6
MOD_COUNT = 6
GLA_HEADS = 4
GLA_RANK = 16
GLA_TAU = 16.0
GLA_CHUNK = 64
GLA_PREPASS_CHUNKS = 4
ROPE_BASE = 10000.0
GRID_W = 64
RNN_BLOCK = 128
RNN_C = 8.0
SSD_HEAD_DIM = 64
SSD_STATE = 128
SSD_GROUPS = 8
SSD_CHUNK = 128

LANES = 128
V7X_VMEM_BYTES = 64 * 1024 * 1024
VMEM_LIMIT = V7X_VMEM_BYTES - 8 * 1024 * 1024


def _cparams(sem):
    return pltpu.CompilerParams(dimension_semantics=sem, vmem_limit_bytes=VMEM_LIMIT)


def _sigmoid(x):
    return 1.0 / (1.0 + jnp.exp(-x))


def _silu(x):
    return x * _sigmoid(x)


def _softplus(x):
    return jnp.maximum(x, 0.0) + jnp.log1p(jnp.exp(-jnp.abs(x)))


def _log_sigmoid(x):
    return jnp.minimum(x, 0.0) - jnp.log1p(jnp.exp(-jnp.abs(x)))


def _gelu_tanh(x):
    return 0.5 * x * (1.0 + jnp.tanh(math.sqrt(2.0 / math.pi) * (x + 0.044715 * (x * x * x))))


def _rms(x, gain):
    ms = jnp.mean(x * x, axis=-1, keepdims=True)
    return x * lax.rsqrt(ms + EPS) * gain


def _split3(x):
    hi = x.astype(BF16)
    r1 = x - hi.astype(F32)
    mid = r1.astype(BF16)
    lo = (r1 - mid.astype(F32)).astype(BF16)
    return hi, mid, lo


def _dot(a, b):
    return jnp.dot(a, b, preferred_element_type=F32)


def _dot_nt(a, b):
    return lax.dot_general(a, b, (((1,), (1,)), ((), ())), preferred_element_type=F32)


def _dot_tn(a, b):
    return lax.dot_general(a, b, (((0,), (0,)), ((), ())), preferred_element_type=F32)


def _dot_exact_rhs(a_bf16, x):
    hi, mid, lo = _split3(x)
    return _dot(a_bf16, hi) + _dot(a_bf16, mid) + _dot(a_bf16, lo)


def _tri_masks(n):
    ri = lax.broadcasted_iota(jnp.int32, (n, n), 0)
    ci = lax.broadcasted_iota(jnp.int32, (n, n), 1)
    return ri >= ci, ri <= ci


def _row_to_col(row):
    cols = []
    for j in range(row.shape[1] // LANES):
        t = jnp.broadcast_to(row[:, j * LANES:(j + 1) * LANES], (16, LANES)).T
        cols.append(jnp.broadcast_to(t[:, 0:1], (LANES, LANES)))
    return jnp.concatenate(cols, axis=0)


def _adaln_kernel(c_ref, w_ref, b_ref, o_ref):
    s = _silu(c_ref[...]).astype(BF16)
    o_ref[...] = _dot(s, w_ref[...].astype(BF16)) + b_ref[...]


def _adaln(cvec, w_ada, b_ada, tn=1024):
    depth, d, n = w_ada.shape
    r = cvec.shape[0]
    return pl.pallas_call(
        _adaln_kernel,
        grid=(depth, n // tn),
        in_specs=[
            pl.BlockSpec((r, d), lambda l, j: (0, 0)),
            pl.BlockSpec((None, d, tn), lambda l, j: (l, 0, j)),
            pl.BlockSpec((None, 1, tn), lambda l, j: (l, 0, j)),
        ],
        out_specs=pl.BlockSpec((None, r, tn), lambda l, j: (l, 0, j)),
        out_shape=jax.ShapeDtypeStruct((depth, r, n), F32),
        compiler_params=_cparams(("arbitrary", "arbitrary")),
        name="adaln",
    )(cvec, w_ada, b_ada.reshape(depth, 1, n))


def _modnorm(x, mod_ref, gain, j):
    shift = mod_ref[3 * j:3 * j + 1, :]
    scale = mod_ref[3 * j + 1:3 * j + 2, :]
    return _rms(x, gain) * (1.0 + scale) + shift


def _mm_in_kernel(*refs, has_small):
    if has_small:
        x_ref, mod_ref, g_ref, w_ref, ws_ref, o_ref, os_ref, h_ref = refs
    else:
        x_ref, mod_ref, g_ref, w_ref, o_ref, h_ref = refs

    @pl.when(pl.program_id(1) == 0)
    def _():
        h_ref[...] = _modnorm(x_ref[...], mod_ref, g_ref[...], 0).astype(BF16)
        if has_small:
            os_ref[...] = _dot(h_ref[...], ws_ref[...])

    o_ref[...] = _dot(h_ref[...], w_ref[...]).astype(o_ref.dtype)


def _mm_in(x, mod, gain, w, w_small=None, tm=1024, tn=1152):
    b, l, d = x.shape
    n = w.shape[1]
    m = b * l
    nb = mod.shape[0]
    tm = min(tm, l if nb > 1 else m)
    tiles_per_batch = l // tm
    mod_idx = (lambda i, j: (i // tiles_per_batch, 0, 0)) if nb > 1 else (lambda i, j: (0, 0, 0))
    has_small = w_small is not None
    in_specs = [
        pl.BlockSpec((tm, d), lambda i, j: (i, 0)),
        pl.BlockSpec((None, MOD_COUNT, d), mod_idx),
        pl.BlockSpec((1, d), lambda i, j: (0, 0)),
        pl.BlockSpec((d, tn), lambda i, j: (0, j)),
    ]
    args = [x.reshape(m, d), mod, gain, w]
    out_specs = [pl.BlockSpec((tm, tn), lambda i, j: (i, j))]
    out_shape = [jax.ShapeDtypeStruct((m, n), BF16)]
    if has_small:
        ns = w_small.shape[1]
        in_specs.append(pl.BlockSpec((d, ns), lambda i, j: (0, 0)))
        args.append(w_small)
        out_specs.append(pl.BlockSpec((tm, ns), lambda i, j: (i, 0)))
        out_shape.append(jax.ShapeDtypeStruct((m, ns), F32))
    res = pl.pallas_call(
        functools.partial(_mm_in_kernel, has_small=has_small),
        grid=(m // tm, n // tn),
        in_specs=in_specs,
        out_specs=out_specs,
        out_shape=out_shape,
        scratch_shapes=[pltpu.VMEM((tm, d), BF16)],
        compiler_params=_cparams(("arbitrary", "arbitrary")),
        name="mm_in",
    )(*args)
    if has_small:
        return res[0].reshape(b, l, n), res[1].reshape(b, l, -1)
    return res[0].reshape(b, l, n)


def _mm_out_kernel(*refs, n_lhs):
    lhs_refs = refs[:n_lhs]
    w_ref, x_ref, mod_ref, g_ref, o_ref = refs[n_lhs:]
    acc = None
    k0 = 0
    for a_ref in lhs_refs:
        kk = a_ref.shape[1]
        part = _dot(a_ref[...].astype(BF16), w_ref[k0:k0 + kk, :])
        acc = part if acc is None else acc + part
        k0 += kk
    gate = mod_ref[2:3, :]
    o_ref[...] = x_ref[...] + gate * _rms(acc, g_ref[...])


def _mm_out(lhs_list, w, x, mod, gain, tm=512):
    b, l, d = x.shape
    m = b * l
    nb = mod.shape[0]
    tm = min(tm, l if nb > 1 else m)
    tiles_per_batch = l // tm
    mod_idx = (lambda i: (i // tiles_per_batch, 0, 0)) if nb > 1 else (lambda i: (0, 0, 0))
    ktot = w.shape[0]
    in_specs = [pl.BlockSpec((tm, a.shape[-1]), lambda i: (i, 0)) for a in lhs_list]
    in_specs += [
        pl.BlockSpec((ktot, d), lambda i: (0, 0), pipeline_mode=pl.Buffered(1)),
        pl.BlockSpec((tm, d), lambda i: (i, 0)),
        pl.BlockSpec((None, MOD_COUNT, d), mod_idx),
        pl.BlockSpec((1, d), lambda i: (0, 0)),
    ]
    out = pl.pallas_call(
        functools.partial(_mm_out_kernel, n_lhs=len(lhs_list)),
        grid=(m // tm,),
        in_specs=in_specs,
        out_specs=pl.BlockSpec((tm, d), lambda i: (i, 0)),
        out_shape=jax.ShapeDtypeStruct((m, d), F32),
        compiler_params=_cparams(("arbitrary",)),
        name="mm_out",
    )(*[a.reshape(m, a.shape[-1]) for a in lhs_list], w, x.reshape(m, d), mod, gain)
    return out.reshape(b, l, d)


def _ffn_kernel(x_ref, mod_ref, g_ref, wg_ref, wu_ref, wd_ref, o_ref, h_ref, acc_ref):
    f = pl.program_id(1)

    @pl.when(f == 0)
    def _():
        h_ref[...] = _modnorm(x_ref[...], mod_ref, g_ref[0:1, :], 1).astype(BF16)
        acc_ref[...] = jnp.zeros_like(acc_ref)

    h = h_ref[...]
    a = _silu(_dot(h, wg_ref[...])) * _dot(h, wu_ref[...])
    acc_ref[...] += _dot(a.astype(BF16), wd_ref[...])

    @pl.when(f == pl.num_programs(1) - 1)
    def _():
        gate = mod_ref[5:6, :]
        o_ref[...] = x_ref[...] + gate * _rms(acc_ref[...], g_ref[1:2, :])


def _ffn(x, mod, gains, wg, wu, wd, tm=512, tf=512):
    b, l, d = x.shape
    m = b * l
    ff = wg.shape[1]
    nb = mod.shape[0]
    tm = min(tm, l if nb > 1 else m)
    tiles_per_batch = l // tm
    mod_idx = (lambda i, f: (i // tiles_per_batch, 0, 0)) if nb > 1 else (lambda i, f: (0, 0, 0))
    out = pl.pallas_call(
        _ffn_kernel,
        grid=(m // tm, ff // tf),
        in_specs=[
            pl.BlockSpec((tm, d), lambda i, f: (i, 0)),
            pl.BlockSpec((None, MOD_COUNT, d), mod_idx),
            pl.BlockSpec((2, d), lambda i, f: (0, 0)),
            pl.BlockSpec((d, tf), lambda i, f: (0, f)),
            pl.BlockSpec((d, tf), lambda i, f: (0, f)),
            pl.BlockSpec((tf, d), lambda i, f: (f, 0)),
        ],
        out_specs=pl.BlockSpec((tm, d), lambda i, f: (i, 0)),
        out_shape=jax.ShapeDtypeStruct((m, d), F32),
        scratch_shapes=[pltpu.VMEM((tm, d), BF16), pltpu.VMEM((tm, d), F32)],
        compiler_params=_cparams(("arbitrary", "arbitrary")),
        name="ffn",
    )(x.reshape(m, d), mod, gains, wg, wu, wd)
    return out.reshape(b, l, d)


def _rope_rotate(x, cos, sin):
    half = x.shape[1] // 2
    swapped = jnp.concatenate(
        [pltpu.roll(x[:, :half], half // 2, axis=1), pltpu.roll(x[:, half:], half // 2, axis=1)], axis=1)
    return x * cos + swapped * sin


def _gla_kernel(*refs, seq, rope, has_s0, want_state):
    it = iter(refs)
    q_ref, k_ref, v_ref, g_ref, lr_ref, wup_ref, bup_ref, gn_ref = [next(it) for _ in range(8)]
    cos_ref = sin_ref = s0_ref = snew_ref = None
    if rope:
        cos_ref, sin_ref = next(it), next(it)
    if has_s0:
        s0_ref = next(it)
    o_ref = next(it)
    if want_state:
        snew_ref = next(it)
    s_ref, oacc_ref, qd_ref, k2_ref, ec_ref = [next(it) for _ in range(5)]

    cs = GLA_CHUNK
    nc = seq // cs
    _, dk, dv = s_ref.shape
    grp = GLA_PREPASS_CHUNKS
    rg = grp * cs
    ri = lax.broadcasted_iota(jnp.int32, (rg, rg), 0)
    ci = lax.broadcasted_iota(jnp.int32, (rg, rg), 1)
    same_chunk = (ri // cs) == (ci // cs)
    masks = (same_chunk & (ri >= ci), same_chunk & (ri <= ci))
    masks_bf = tuple(jnp.where(m, 1.0, 0.0).astype(BF16) for m in masks)

    def prepass(gi, carry):
        rows = pl.ds(pl.multiple_of(gi * rg, rg), rg)
        q = q_ref[rows, :].astype(F32) * (dk ** -0.5)
        k = k_ref[rows, :].astype(F32)
        if rope:
            cos, sin = cos_ref[rows, :], sin_ref[rows, :]
            q = _rope_rotate(q, cos, sin)
            k = _rope_rotate(k, cos, sin)
        v = v_ref[rows, :]
        lr = lr_ref[rows, :]
        intra = None
        for dirn in range(2):
            la = _log_sigmoid(_dot(lr, wup_ref[dirn]) + bup_ref[dirn]) * (1.0 / GLA_TAU)
            b = _dot_exact_rhs(masks_bf[dirn], la)
            last = cs - 1 if dirn == 0 else 0
            totals = [b[j * cs + last:j * cs + last + 1, :] for j in range(grp)]
            b_all = jnp.concatenate([jnp.broadcast_to(t, (cs, dk)) for t in totals], axis=0)
            qd = (q * jnp.exp(b)).astype(BF16)
            kd = (k * jnp.exp(-b)).astype(BF16)
            qd_ref[dirn, rows, :] = qd
            k2_ref[dirn, rows, :] = (k * jnp.exp(b_all - b)).astype(BF16)
            for j in range(grp):
                ec_ref[dirn, gi * grp + j] = _row_to_col(jnp.exp(totals[j]))
            sc = jnp.where(masks[dirn], _dot_nt(qd, kd), 0.0)
            part = _dot(sc.astype(BF16), v)
            intra = part if intra is None else intra + part
        oacc_ref[rows, :] = intra
        return carry

    lax.fori_loop(0, seq // rg, prepass, 0, unroll=2 if seq // rg > 1 else 1)

    def step(c, dirn):
        rows = pl.ds(pl.multiple_of(c * cs, cs), cs)
        s_old = s_ref[dirn]
        inter = _dot(qd_ref[dirn, rows, :], s_old.astype(BF16))
        decay = jnp.concatenate([ec_ref[dirn, c]] * (dv // LANES), axis=1)
        s_ref[dirn] = decay * s_old + _dot_tn(k2_ref[dirn, rows, :], v_ref[rows, :])
        return rows, inter

    for dirn in range(2):
        if has_s0:
            s_ref[dirn] = s0_ref[dirn]
        else:
            s_ref[dirn] = jnp.zeros((dk, dv), F32)

    def first_half(i, carry):
        for dirn, c in ((0, i), (1, nc - 1 - i)):
            rows, inter = step(c, dirn)
            oacc_ref[rows, :] += inter
        return carry

    def second_half(i, carry):
        for dirn, c in ((0, i), (1, nc - 1 - i)):
            rows, inter = step(c, dirn)
            o = _rms(oacc_ref[rows, :] + inter, gn_ref[...]) * _silu(g_ref[rows, :].astype(F32))
            o_ref[rows, :] = o.astype(o_ref.dtype)
        return carry

    lax.fori_loop(0, nc // 2, first_half, 0)
    lax.fori_loop(nc // 2, nc, second_half, 0)
    if want_state:
        for dirn in range(2):
            snew_ref[dirn] = s_ref[dirn]


def _gla(proj, col, wup, bup, gn, rope_tabs, s0, want_state):
    b, l, _ = proj.shape
    h = GLA_HEADS
    dk = wup.shape[2] // h
    dv = gn.shape[1] // h
    nc = l // GLA_CHUNK
    rope = rope_tabs is not None
    has_s0 = s0 is not None
    in_specs = [
        pl.BlockSpec((None, l, dk), lambda bi, hi: (bi, 0, col["q"] // dk + hi)),
        pl.BlockSpec((None, l, dk), lambda bi, hi: (bi, 0, col["k"] // dk + hi)),
        pl.BlockSpec((None, l, dv), lambda bi, hi: (bi, 0, col["v"] // dv + hi)),
        pl.BlockSpec((None, l, dv), lambda bi, hi: (bi, 0, col["g"] // dv + hi)),
        pl.BlockSpec((None, l, LANES), lambda bi, hi: (bi, 0, col["lr"] // LANES)),
        pl.BlockSpec((2, LANES, dk), lambda bi, hi: (0, 0, hi)),
        pl.BlockSpec((2, 1, dk), lambda bi, hi: (0, 0, hi)),
        pl.BlockSpec((1, dv), lambda bi, hi: (0, hi)),
    ]
    args = [proj, proj, proj, proj, proj, wup, bup, gn]
    if rope:
        in_specs += [pl.BlockSpec((l, dk), lambda bi, hi: (0, 0), pipeline_mode=pl.Buffered(1))] * 2
        args += list(rope_tabs)
    if has_s0:
        in_specs.append(pl.BlockSpec((None, 2, None, dk, dv), lambda bi, hi: (bi, 0, hi, 0, 0)))
        args.append(s0)
    out_specs = [pl.BlockSpec((None, l, dv), lambda bi, hi: (bi, 0, hi))]
    out_shape = [jax.ShapeDtypeStruct((b, l, h * dv), BF16)]
    if want_state:
        out_specs.append(pl.BlockSpec((None, 2, None, dk, dv), lambda bi, hi: (bi, 0, hi, 0, 0)))
        out_shape.append(jax.ShapeDtypeStruct((b, 2, h, dk, dv), F32))
    res = pl.pallas_call(
        functools.partial(_gla_kernel, seq=l, rope=rope, has_s0=has_s0, want_state=want_state),
        grid=(b, h),
        in_specs=in_specs,
        out_specs=out_specs,
        out_shape=out_shape,
        scratch_shapes=[
            pltpu.VMEM((2, dk, dv), F32), pltpu.VMEM((l, dv), F32),
            pltpu.VMEM((2, l, dk), BF16), pltpu.VMEM((2, l, dk), BF16),
            pltpu.VMEM((2, nc, dk, LANES), F32),
        ],
        compiler_params=_cparams(("arbitrary", "arbitrary")),
        name="gla",
    )(*args)
    return res if want_state else res[0]


def _tile_scan(a, u, reverse):
    n = a.shape[0]
    rid = lax.broadcasted_iota(jnp.int32, a.shape, 0)
    for d in (1, 2, 4):
        shift, m = (n - d, rid < n - d) if reverse else (d, rid >= d)
        a_s = pltpu.roll(a, shift, axis=0)
        u_s = pltpu.roll(u, shift, axis=0)
        u = jnp.where(m, u + a * u_s, u)
        a = jnp.where(m, a * a_s, a)
    return a, u


def _conv4(pad_ref, r0, n, halo, w_ref, b_ref):
    xw = pad_ref[pl.ds(r0, n + 2 * halo), :]
    tot = n + 2 * halo
    return (w_ref[0:1, :] * pltpu.roll(xw, 2, axis=0)[halo:halo + n]
            + w_ref[1:2, :] * pltpu.roll(xw, 1, axis=0)[halo:halo + n]
            + w_ref[2:3, :] * xw[halo:halo + n]
            + w_ref[3:4, :] * pltpu.roll(xw, tot - 1, axis=0)[halo:halo + n]
            + b_ref[...])


def _fill_padded(pad_ref, src_ref, seq, halo, step):
    wd = pad_ref.shape[1]
    pad_ref[0:halo, :] = jnp.zeros((halo, wd), F32)
    pad_ref[seq + halo:seq + 2 * halo, :] = jnp.zeros((halo, wd), F32)

    def body(i, carry):
        r0 = pl.multiple_of(i * step, step)
        pad_ref[pl.ds(r0 + halo, step), :] = src_ref[pl.ds(r0, step), :].astype(F32)
        return carry

    lax.fori_loop(0, seq // step, body, 0)


def _rglru_kernel(*refs, seq, has_h0, want_state, tc):
    it = iter(refs)
    xr_ref, yr_ref, cw_ref, cb_ref, wg_ref, bg_ref, lam_ref = [next(it) for _ in range(7)]
    h0_ref = next(it) if has_h0 else None
    y_ref = next(it)
    hl_ref = next(it) if want_state else None
    xpad_ref, af_ref, uf_ref, ab_ref, ub_ref, gy_ref = [next(it) for _ in range(6)]

    wb = xr_ref.shape[1]
    nblk = wb // RNN_BLOCK
    halo = 8
    _fill_padded(xpad_ref, xr_ref, seq, halo, tc)
    ls = _log_sigmoid(lam_ref[...])

    def gates(c, carry):
        r0 = pl.multiple_of(c * tc, tc)
        rows = pl.ds(r0, tc)
        xc = _conv4(xpad_ref, r0, tc, halo, cw_ref, cb_ref)
        gy_ref[rows, :] = _gelu_tanh(yr_ref[rows, :].astype(F32))
        for j in range(nblk):
            lanes = slice(j * RNN_BLOCK, (j + 1) * RNN_BLOCK)
            xcj = xc[:, lanes]
            gt = _dot(xcj.astype(BF16), wg_ref[j]) + bg_ref[j]
            for dirn, (a_ref, u_ref) in enumerate(((af_ref, uf_ref), (ab_ref, ub_ref))):
                r = _sigmoid(gt[:, dirn * RNN_BLOCK:(dirn + 1) * RNN_BLOCK])
                i = _sigmoid(gt[:, (2 + dirn) * RNN_BLOCK:(3 + dirn) * RNN_BLOCK])
                log_a = RNN_C * r * ls[dirn:dirn + 1, lanes]
                a = jnp.exp(log_a)
                u = jnp.sqrt(-jnp.tanh(log_a) * (a * a + 1.0)) * i * xcj
                a_ref[rows, lanes] = a
                u_ref[rows, lanes] = u
        return carry

    lax.fori_loop(0, seq // tc, gates, 0)

    h0f = h0_ref[0:1, :] if has_h0 else jnp.zeros((1, wb), F32)
    h0b = h0_ref[1:2, :] if has_h0 else jnp.zeros((1, wb), F32)

    def fwd(t, hc):
        rows = pl.ds(pl.multiple_of(t * 8, 8), 8)
        a, u = _tile_scan(af_ref[rows, :], uf_ref[rows, :], False)
        hh = u + a * hc
        uf_ref[rows, :] = hh
        return hh[7:8, :]

    hlf = lax.fori_loop(0, seq // 8, fwd, h0f, unroll=4)

    n16 = seq // 16

    def bwd(i, hc):
        r16 = pl.multiple_of((n16 - 1 - i) * 16, 16)
        lo, hi = pl.ds(r16, 8), pl.ds(r16 + 8, 8)
        a1, u1 = _tile_scan(ab_ref[hi, :], ub_ref[hi, :], True)
        h1 = u1 + a1 * hc
        a0, u0 = _tile_scan(ab_ref[lo, :], ub_ref[lo, :], True)
        h0 = u0 + a0 * h1[0:1, :]
        y = jnp.concatenate([(uf_ref[lo, :] + h0) * gy_ref[lo, :], (uf_ref[hi, :] + h1) * gy_ref[hi, :]], axis=0)
        y_ref[pl.ds(r16, 16), :] = y.astype(y_ref.dtype)
        return h0[0:1, :]

    hlb = lax.fori_loop(0, n16, bwd, h0b, unroll=2)
    if want_state:
        hl_ref[0:1, :] = hlf
        hl_ref[1:2, :] = hlb


def _rglru(proj, col, cw, cb, wg, bg, lam, h0, want_state, wb=512, tc=256):
    b, l, _ = proj.shape
    w = cw.shape[1]
    tc = min(tc, l)
    nblk = wb // RNN_BLOCK
    has_h0 = h0 is not None
    in_specs = [
        pl.BlockSpec((None, l, wb), lambda bi, j: (bi, 0, col["xr"] // wb + j)),
        pl.BlockSpec((None, l, wb), lambda bi, j: (bi, 0, col["yr"] // wb + j)),
        pl.BlockSpec((4, wb), lambda bi, j: (0, j)),
        pl.BlockSpec((1, wb), lambda bi, j: (0, j)),
        pl.BlockSpec((nblk, RNN_BLOCK, 4 * RNN_BLOCK), lambda bi, j: (j, 0, 0)),
        pl.BlockSpec((nblk, 1, 4 * RNN_BLOCK), lambda bi, j: (j, 0, 0)),
        pl.BlockSpec((2, wb), lambda bi, j: (0, j)),
    ]
    args = [proj, proj, cw, cb, wg, bg, lam]
    if has_h0:
        in_specs.append(pl.BlockSpec((None, 2, wb), lambda bi, j: (bi, 0, j)))
        args.append(h0)
    out_specs = [pl.BlockSpec((None, l, wb), lambda bi, j: (bi, 0, j))]
    out_shape = [jax.ShapeDtypeStruct((b, l, w), BF16)]
    if want_state:
        out_specs.append(pl.BlockSpec((None, 2, wb), lambda bi, j: (bi, 0, j)))
        out_shape.append(jax.ShapeDtypeStruct((b, 2, w), F32))
    res = pl.pallas_call(
        functools.partial(_rglru_kernel, seq=l, has_h0=has_h0, want_state=want_state, tc=tc),
        grid=(b, w // wb),
        in_specs=in_specs,
        out_specs=out_specs,
        out_shape=out_shape,
        scratch_shapes=[pltpu.VMEM((l + 16, wb), F32)] + [pltpu.VMEM((l, wb), F32)] * 5,
        compiler_params=_cparams(("arbitrary", "arbitrary")),
        name="rglru",
    )(*args)
    return res if want_state else res[0]


def _ssd_dt_kernel(dt_ref, bias_ref, alog_ref, cf_ref, cb_ref, qf_ref, qb_ref, cfr_ref, cbr_ref, qfr_ref, qbr_ref,
                   *, seq):
    cs = SSD_CHUNK
    tril, triu = _tri_masks(cs)
    tril_bf = jnp.where(tril, 1.0, 0.0).astype(BF16)
    triu_bf = jnp.where(triu, 1.0, 0.0).astype(BF16)
    a = -jnp.exp(alog_ref[...])

    def body(c, carry):
        rows = pl.ds(pl.multiple_of(c * cs, cs), cs)
        dt = _softplus(dt_ref[rows, :] + bias_ref[...])
        la = dt * a
        log_dt = jnp.log(dt)
        cf = _dot_exact_rhs(tril_bf, la)
        cb = _dot_exact_rhs(triu_bf, la)
        for val, col_ref, row_ref in ((cf, cf_ref, cfr_ref), (cb, cb_ref, cbr_ref),
                                      (cf - log_dt, qf_ref, qfr_ref), (cb - log_dt, qb_ref, qbr_ref)):
            col_ref[rows, :] = val
            row_ref[c] = val.T
        return carry

    lax.fori_loop(0, seq // cs, body, 0)


def _ssd_dt(dt_raw, bias, alog):
    b, l, _ = dt_raw.shape
    nc = l // SSD_CHUNK
    cshp = jax.ShapeDtypeStruct((b, l, LANES), F32)
    cspec = pl.BlockSpec((None, l, LANES), lambda bi: (bi, 0, 0))
    rshp = jax.ShapeDtypeStruct((b, nc, LANES, SSD_CHUNK), F32)
    rspec = pl.BlockSpec((None, nc, LANES, SSD_CHUNK), lambda bi: (bi, 0, 0, 0))
    return pl.pallas_call(
        functools.partial(_ssd_dt_kernel, seq=l),
        grid=(b,),
        in_specs=[
            pl.BlockSpec((None, l, LANES), lambda bi: (bi, 0, 0)),
            pl.BlockSpec((1, LANES), lambda bi: (0, 0)),
            pl.BlockSpec((1, LANES), lambda bi: (0, 0)),
        ],
        out_specs=[cspec] * 4 + [rspec] * 4,
        out_shape=[cshp] * 4 + [rshp] * 4,
        compiler_params=_cparams(("arbitrary",)),
        name="ssd_dt",
    )(dt_raw, bias, alog)


def _ssd_kernel(*refs, seq, has_s0, want_state):
    it = iter(refs)
    (z_ref, x_ref, bm_ref, cm_ref, cfc_ref, cbc_ref, qfc_ref, qbc_ref, cfr_ref, cbr_ref, qfr_ref, qbr_ref,
     cwx_ref, cwb_ref, cwc_ref, cbx_ref, cbb_ref, cbc2_ref, dsk_ref, gn_ref) = [next(it) for _ in range(20)]
    s0_ref = next(it) if has_s0 else None
    y_ref = next(it)
    snew_ref = next(it) if want_state else None
    (xpad_ref, bpad_ref, cpad_ref, xs_ref, bt_ref, cs_ref, cbm_ref, yacc_ref, st_ref) = [next(it) for _ in range(9)]
    cum_cols, cum_rows = (cfc_ref, cbc_ref), (cfr_ref, cbr_ref)
    q_cols, q_rows = (qfc_ref, qbc_ref), (qfr_ref, qbr_ref)

    cs = SSD_CHUNK
    nc = seq // cs
    hp = SSD_HEAD_DIM
    gw = x_ref.shape[1]
    hg = gw // hp
    npair = gw // LANES
    halo = 8
    _fill_padded(xpad_ref, x_ref, seq, halo, cs)
    _fill_padded(bpad_ref, bm_ref, seq, halo, cs)
    _fill_padded(cpad_ref, cm_ref, seq, halo, cs)

    def conv(c, carry):
        r0 = pl.multiple_of(c * cs, cs)
        rows = pl.ds(r0, cs)
        xs_ref[rows, :] = _silu(_conv4(xpad_ref, r0, cs, halo, cwx_ref, cbx_ref))
        bc = _silu(_conv4(bpad_ref, r0, cs, halo, cwb_ref, cbb_ref))
        cc = _silu(_conv4(cpad_ref, r0, cs, halo, cwc_ref, cbc2_ref)).astype(BF16)
        bt_ref[c] = bc.T.astype(BF16)
        cs_ref[rows, :] = cc
        cbm_ref[rows, :] = _dot_nt(cc, bc.astype(BF16))
        return carry

    lax.fori_loop(0, nc, conv, 0, unroll=2)

    masks = _tri_masks(cs)
    lane = lax.broadcasted_iota(jnp.int32, (cs, LANES), 1)
    first_of_pair = lane < hp
    gbase = pl.program_id(1) * (2 * hg)

    def chunk(c, dirn):
        rows = pl.ds(pl.multiple_of(c * cs, cs), cs)
        xs = xs_ref[rows, :]
        c_c = cs_ref[rows, :]
        cbm = cbm_ref[rows, :]
        cum_t = cum_cols[dirn][rows, :]
        q_t = q_cols[dirn][rows, :]
        q_r = q_rows[dirn][c]
        last = cs - 1 if dirn == 0 else 0
        end_lane = cum_t[last:last + 1, :]
        w_all = jnp.exp(end_lane - q_t)
        e_end = jnp.exp(jnp.broadcast_to(end_lane, (8, LANES)))
        s_old = st_ref[dirn]
        y_state = _dot(c_c, s_old.astype(BF16))
        tiles, xw_tiles, end_tiles = [], [], []
        for m in range(npair):
            h0 = dirn * hg + 2 * m
            idx = gbase + h0 + lane // hp
            xp = xs[:, m * LANES:(m + 1) * LANES]
            xw_tiles.append((xp * jnp.take_along_axis(w_all, idx, axis=1)).astype(BF16))
            end_tiles.append(jnp.take_along_axis(e_end, idx[0:8], axis=1)[0:1])
            rhs = jnp.concatenate([jnp.where(first_of_pair, xp, 0.0), jnp.where(first_of_pair, 0.0, xp)],
                                  axis=0).astype(BF16)
            lhs, ecum = [], []
            for jj in range(2):
                col = jnp.take_along_axis(cum_t, jnp.zeros_like(lane) + (gbase + h0 + jj), axis=1)
                seg = col - q_r[h0 + jj:h0 + jj + 1, :]
                lhs.append((cbm * jnp.exp(jnp.where(masks[dirn], seg, -jnp.inf))).astype(BF16))
                ecum.append(jnp.exp(col))
            y_p = _dot(jnp.concatenate(lhs, axis=1), rhs)
            y_p = y_p + y_state[:, m * LANES:(m + 1) * LANES] * jnp.where(first_of_pair, ecum[0], ecum[1])
            tiles.append(y_p)
        xw = jnp.concatenate(xw_tiles, axis=1)
        st_ref[dirn] = s_old * jnp.concatenate(end_tiles, axis=1) + _dot(bt_ref[c], xw)
        return rows, jnp.concatenate(tiles, axis=1)

    for dirn in range(2):
        if has_s0:
            st_ref[dirn] = s0_ref[dirn].T
        else:
            st_ref[dirn] = jnp.zeros(st_ref.shape[1:], F32)

    def finish(rows, y):
        y = y + dsk_ref[...] * xs_ref[rows, :]
        y = y * _silu(z_ref[rows, :].astype(F32))
        y_ref[rows, :] = _rms(y, gn_ref[...]).astype(y_ref.dtype)

    def first_half(i, carry):
        for dirn, c in ((0, i), (1, nc - 1 - i)):
            rows, y = chunk(c, dirn)
            yacc_ref[rows, :] = y
        return carry

    def second_half(i, carry):
        for dirn, c in ((0, i), (1, nc - 1 - i)):
            rows, y = chunk(c, dirn)
            finish(rows, yacc_ref[rows, :] + y)
        return carry

    lax.fori_loop(0, nc // 2, first_half, 0)
    lax.fori_loop(nc // 2, nc, second_half, 0)
    if want_state:
        for dirn in range(2):
            snew_ref[dirn] = st_ref[dirn].T


def _ssd(proj, col, cums, cw, cb, dsk, gn, s0, want_state):
    b, l, _ = proj.shape
    g = SSD_GROUPS
    inner = dsk.shape[1]
    gw = inner // g
    ns = SSD_STATE
    nc = l // SSD_CHUNK
    hg2 = 2 * gw // SSD_HEAD_DIM
    has_s0 = s0 is not None
    bcol0 = inner // ns
    ccol0 = bcol0 + g
    assert nc % 2 == 0
    colspec = pl.BlockSpec((None, l, LANES), lambda bi, gi: (bi, 0, 0))
    rowspec = pl.BlockSpec((None, nc, hg2, SSD_CHUNK), lambda bi, gi: (bi, 0, gi, 0))
    in_specs = [
        pl.BlockSpec((None, l, gw), lambda bi, gi: (bi, 0, col["z"] // gw + gi)),
        pl.BlockSpec((None, l, gw), lambda bi, gi: (bi, 0, col["x"] // gw + gi)),
        pl.BlockSpec((None, l, ns), lambda bi, gi: (bi, 0, col["B"] // ns + gi)),
        pl.BlockSpec((None, l, ns), lambda bi, gi: (bi, 0, col["C"] // ns + gi)),
        colspec, colspec, colspec, colspec, rowspec, rowspec, rowspec, rowspec,
        pl.BlockSpec((4, gw), lambda bi, gi: (0, gi)),
        pl.BlockSpec((4, ns), lambda bi, gi: (0, bcol0 + gi)),
        pl.BlockSpec((4, ns), lambda bi, gi: (0, ccol0 + gi)),
        pl.BlockSpec((1, gw), lambda bi, gi: (0, gi)),
        pl.BlockSpec((1, ns), lambda bi, gi: (0, bcol0 + gi)),
        pl.BlockSpec((1, ns), lambda bi, gi: (0, ccol0 + gi)),
        pl.BlockSpec((1, gw), lambda bi, gi: (0, gi)),
        pl.BlockSpec((1, gw), lambda bi, gi: (0, gi)),
    ]
    args = [proj, proj, proj, proj, *cums, cw, cw, cw, cb, cb, cb, dsk, gn]
    if has_s0:
        in_specs.append(pl.BlockSpec((None, 2, gw, ns), lambda bi, gi: (bi, 0, gi, 0)))
        args.append(s0)
    out_specs = [pl.BlockSpec((None, l, gw), lambda bi, gi: (bi, 0, gi))]
    out_shape = [jax.ShapeDtypeStruct((b, l, inner), BF16)]
    if want_state:
        out_specs.append(pl.BlockSpec((None, 2, gw, ns), lambda bi, gi: (bi, 0, gi, 0)))
        out_shape.append(jax.ShapeDtypeStruct((b, 2, inner, ns), F32))
    res = pl.pallas_call(
        functools.partial(_ssd_kernel, seq=l, has_s0=has_s0, want_state=want_state),
        grid=(b, g),
        in_specs=in_specs,
        out_specs=out_specs,
        out_shape=out_shape,
        scratch_shapes=[
            pltpu.VMEM((l + 16, gw), F32), pltpu.VMEM((l + 16, ns), F32), pltpu.VMEM((l + 16, ns), F32),
            pltpu.VMEM((l, gw), F32), pltpu.VMEM((nc, ns, SSD_CHUNK), BF16), pltpu.VMEM((l, ns), BF16),
            pltpu.VMEM((l, SSD_CHUNK), F32), pltpu.VMEM((l, gw), F32), pltpu.VMEM((2, ns, gw), F32),
        ],
        compiler_params=_cparams(("arbitrary", "arbitrary")),
        name="ssd",
    )(*args)
    return res if want_state else res[0]


def _prep_even(w_in, w_out, w_up, b_up, w_r, b_r, w_i, b_i):
    d = w_in.shape[0]
    qk = w_up.shape[2]
    gv = (w_in.shape[1] - 2 * qk - 2 * GLA_RANK) // 4
    sizes = (qk, qk, gv, gv, 2 * GLA_RANK, gv, gv)
    offs = [0]
    for s in sizes:
        offs.append(offs[-1] + s)
    q, k, v, g, lr, xr, yr = (w_in[:, offs[i]:offs[i + 1]] for i in range(7))
    pad = jnp.zeros((d, LANES - 2 * GLA_RANK), w_in.dtype)
    w = jnp.concatenate([q, k, v, g, xr, yr, lr, pad], axis=1).astype(BF16)
    col = {"q": 0, "k": qk, "v": 2 * qk, "g": 2 * qk + gv, "xr": 2 * qk + 2 * gv, "yr": 2 * qk + 3 * gv,
           "lr": 2 * qk + 4 * gv}
    wup = jnp.zeros((2, LANES, qk), F32)
    wup = wup.at[0, 0:GLA_RANK].set(w_up[0]).at[1, GLA_RANK:2 * GLA_RANK].set(w_up[1])
    nblk = w_r.shape[1]
    wg = jnp.concatenate([w_r[0], w_r[1], w_i[0], w_i[1]], axis=-1).astype(BF16)
    bg = jnp.concatenate([b.reshape(nblk, 1, RNN_BLOCK) for b in (b_r[0], b_r[1], b_i[0], b_i[1])], axis=-1)
    return {
        "w_in": w, "col": col, "w_out": w_out.astype(BF16),
        "wup": wup.astype(BF16), "bup": b_up.reshape(2, 1, qk), "wg": wg, "bg": bg,
    }


def _group_major(a, heads):
    g = SSD_GROUPS
    hg = heads // g
    lead = a.shape[:-1]
    return jnp.swapaxes(a.reshape(lead + (2, g, hg)), -3, -2).reshape(lead + (2 * heads,))


def _prep_odd(w_in, w_out, dt_bias, a_log, d_skip):
    heads = d_skip.shape[0]
    inner = heads * SSD_HEAD_DIM
    xbc = inner + 2 * SSD_GROUPS * SSD_STATE
    dtw = _group_major(w_in[:, inner + xbc:], heads).astype(BF16)
    col = {"z": 0, "x": inner, "B": 2 * inner, "C": 2 * inner + SSD_GROUPS * SSD_STATE}
    return {
        "w_in": w_in[:, :inner + xbc].astype(BF16), "w_dt": dtw, "col": col, "w_out": w_out.astype(BF16),
        "dt_bias": _group_major(dt_bias.reshape(1, 2 * heads), heads),
        "a_log": _group_major(a_log.reshape(1, 2 * heads), heads),
        "dsk": jnp.repeat(d_skip, SSD_HEAD_DIM).reshape(1, inner),
    }


def _rope_tables(seq, dk):
    nf = dk // 4
    inv = ROPE_BASE ** (-jnp.arange(nf, dtype=F32) / nf)
    t = jnp.arange(seq)
    ang_r = (t // GRID_W).astype(F32)[:, None] * inv
    ang_c = (t % GRID_W).astype(F32)[:, None] * inv
    cos = jnp.concatenate([jnp.cos(ang_r)] * 2 + [jnp.cos(ang_c)] * 2, axis=1)
    sin = jnp.concatenate([-jnp.sin(ang_r), jnp.sin(ang_r), -jnp.sin(ang_c), jnp.sin(ang_c)], axis=1)
    return cos, sin


def _even_mixer(x, mod, gains, p, gla_gn, cw, cb, lam, s0_gla, h0_rnn, rope_tabs, want_state):
    proj = _mm_in(x, mod, gains[0:1], p["w_in"])
    gla = _gla(proj, p["col"], p["wup"], p["bup"], gla_gn, rope_tabs, s0_gla, want_state)
    rnn = _rglru(proj, p["col"], cw, cb, p["wg"], p["bg"], lam, h0_rnn, want_state)
    if want_state:
        (o, sg), (yr, sr) = gla, rnn
    else:
        (o, sg), (yr, sr) = (gla, None), (rnn, None)
    y = _mm_out([o, yr], p["w_out"], x, mod, gains[1:2])
    return y, sg, sr


def _odd_mixer(x, mod, gains, p, cw, cb, gn, s0, want_state):
    proj, dt_raw = _mm_in(x, mod, gains[0:1], p["w_in"], p["w_dt"], tn=1024)
    cums = _ssd_dt(dt_raw, p["dt_bias"], p["a_log"])
    res = _ssd(proj, p["col"], cums, cw, cb, p["dsk"], gn, s0, want_state)
    yv, ss = res if want_state else (res, None)
    y = _mm_out([yv], p["w_out"], x, mod, gains[1:2])
    return y, ss


def kernel(x_prompt, x_sample, state_gla, state_rglru, state_ssd, c, c_ctx, w_ada, b_ada, norm_g, ev_w_in, ev_w_out,
           gla_w_up, gla_b_up, gla_norm_g, rnn_conv_w, rnn_conv_b, rnn_w_r, rnn_b_r, rnn_w_i, rnn_b_i, rnn_lam,
           od_w_in, od_w_out, ssd_conv_w, ssd_conv_b, ssd_dt_bias, ssd_a_log, ssd_d, ssd_norm_g,
           ffn_w_gate, ffn_w_up, ffn_w_down):
    depth, d = norm_g.shape[0], norm_g.shape[2]
    nb = c.shape[0]
    rows = -(-(nb + 1) // 8) * 8
    cvec = jnp.zeros((rows, d), F32).at[0].set(c_ctx).at[1:nb + 1].set(c)
    mods = _adaln(cvec, w_ada, b_ada).reshape(depth, rows, MOD_COUNT, d)
    rope_tabs = _rope_tables(x_sample.shape[1], gla_w_up.shape[3] // GLA_HEADS)

    yp, ys = x_prompt, x_sample
    new_gla, new_rnn, new_ssd = [], [], []
    for l in range(depth):
        mod_p, mod_s = mods[l, 0:1], mods[l, 1:nb + 1]
        gains = norm_g[l]
        if l % 2 == 0:
            e = l // 2
            p = _prep_even(ev_w_in[e], ev_w_out[e], gla_w_up[e], gla_b_up[e], rnn_w_r[e], rnn_b_r[e],
                           rnn_w_i[e], rnn_b_i[e])
            gn = gla_norm_g[e].reshape(1, -1)
            cw, cb, lam = rnn_conv_w[e], rnn_conv_b[e].reshape(1, -1), rnn_lam[e]
            yp, sg, sr = _even_mixer(yp, mod_p, gains, p, gn, cw, cb, lam, None, None, None, True)
            ys, _, _ = _even_mixer(ys, mod_s, gains, p, gn, cw, cb, lam, state_gla[:, e], state_rglru[:, e],
                                   rope_tabs, False)
            new_gla.append(sg)
            new_rnn.append(sr)
        else:
            o = l // 2
            p = _prep_odd(od_w_in[o], od_w_out[o], ssd_dt_bias[o], ssd_a_log[o], ssd_d[o])
            cw, cb, gn = ssd_conv_w[o], ssd_conv_b[o].reshape(1, -1), ssd_norm_g[o].reshape(1, -1)
            inner, ns = gn.shape[1], SSD_STATE
            s0 = state_ssd[:, o].reshape(nb, 2, inner, ns)
            yp, ss = _odd_mixer(yp, mod_p, gains, p, cw, cb, gn, None, True)
            ys, _ = _odd_mixer(ys, mod_s, gains, p, cw, cb, gn, s0, False)
            new_ssd.append(ss.reshape(ss.shape[0], 2, inner // SSD_HEAD_DIM, SSD_HEAD_DIM, ns))
        wg, wu, wd = ffn_w_gate[l].astype(BF16), ffn_w_up[l].astype(BF16), ffn_w_down[l].astype(BF16)
        yp = _ffn(yp, mod_p, gains[2:4], wg, wu, wd)
        ys = _ffn(ys, mod_s, gains[2:4], wg, wu, wd)
    return (yp, ys, jnp.stack(new_gla, axis=1), jnp.stack(new_rnn, axis=1), jnp.stack(new_ssd, axis=1))
```

```python
import functools
import math

import jax
import jax.numpy as jnp
from jax import lax
from jax.experimental import pallas as pl
from jax.experimental.pallas import tpu as pltpu

F32 = jnp.float32
BF16 = jnp.bfloat16

EPS = 1e-6
MOD_COUNT = 6
GLA_HEADS = 4
GLA_RANK = 16
GLA_TAU = 16.0
GLA_CHUNK = 64
GLA_PREPASS_CHUNKS = 4
ROPE_BASE = 10000.0
GRID_W = 64
RNN_BLOCK = 128
RNN_C = 8.0
SSD_HEAD_DIM = 64
SSD_STATE = 128
SSD_GROUPS = 8
SSD_CHUNK = 128

LOG2_E = math.log2(math.e)
LANES = 128
V7X_VMEM_BYTES = 64 * 1024 * 1024
VMEM_LIMIT = V7X_VMEM_BYTES - 8 * 1024 * 1024


def _cparams(sem):
    return pltpu.CompilerParams(dimension_semantics=sem, vmem_limit_bytes=VMEM_LIMIT)


def _sigmoid(x):
    return 1.0 / (1.0 + jnp.exp(-x))


def _silu(x):
    return x * _sigmoid(x)


def _softplus(x):
    return jnp.maximum(x, 0.0) + jnp.log1p(jnp.exp(-jnp.abs(x)))


def _log_sigmoid(x):
    return jnp.minimum(x, 0.0) - jnp.log1p(jnp.exp(-jnp.abs(x)))


def _gelu_tanh(x):
    return 0.5 * x * (1.0 + jnp.tanh(math.sqrt(2.0 / math.pi) * (x + 0.044715 * (x * x * x))))


def _rms(x, gain):
    ms = jnp.mean(x * x, axis=-1, keepdims=True)
    return x * lax.rsqrt(ms + EPS) * gain


def _split3(x):
    hi = x.astype(BF16)
    r1 = x - hi.astype(F32)
    mid = r1.astype(BF16)
    lo = (r1 - mid.astype(F32)).astype(BF16)
    return hi, mid, lo


def _dot(a, b):
    return jnp.dot(a, b, preferred_element_type=F32)


def _dot_nt(a, b):
    return lax.dot_general(a, b, (((1,), (1,)), ((), ())), preferred_element_type=F32)


def _dot_tn(a, b):
    return lax.dot_general(a, b, (((0,), (0,)), ((), ())), preferred_element_type=F32)


def _dot_exact_rhs(a_bf16, x):
    hi, mid, lo = _split3(x)
    return _dot(a_bf16, hi) + _dot(a_bf16, mid) + _dot(a_bf16, lo)


def _tri_masks(n):
    ri = lax.broadcasted_iota(jnp.int32, (n, n), 0)
    ci = lax.broadcasted_iota(jnp.int32, (n, n), 1)
    return ri >= ci, ri <= ci


def _row_halves(n):
    return (slice(0, n // 2), slice(n // 2, n))


def _row_to_col(row):
    cols = []
    for j in range(row.shape[1] // LANES):
        t = jnp.broadcast_to(row[:, j * LANES:(j + 1) * LANES], (16, LANES)).T
        cols.append(jnp.broadcast_to(t[:, 0:1], (LANES, LANES)))
    return jnp.concatenate(cols, axis=0)


def _adaln_kernel(c_ref, w_ref, b_ref, o_ref):
    s = _silu(c_ref[...]).astype(BF16)
    o_ref[...] = _dot(s, w_ref[...].astype(BF16)) + b_ref[...]


def _adaln(cvec, w_ada, b_ada, tn=1024):
    depth, d, n = w_ada.shape
    r = cvec.shape[0]
    return pl.pallas_call(
        _adaln_kernel,
        grid=(depth, n // tn),
        in_specs=[
            pl.BlockSpec((r, d), lambda l, j: (0, 0)),
            pl.BlockSpec((None, d, tn), lambda l, j: (l, 0, j)),
            pl.BlockSpec((None, 1, tn), lambda l, j: (l, 0, j)),
        ],
        out_specs=pl.BlockSpec((None, r, tn), lambda l, j: (l, 0, j)),
        out_shape=jax.ShapeDtypeStruct((depth, r, n), F32),
        compiler_params=_cparams(("arbitrary", "arbitrary")),
        name="adaln",
    )(cvec, w_ada, b_ada.reshape(depth, 1, n))


def _modnorm(x, mod_ref, gain, j):
    shift = mod_ref[3 * j:3 * j + 1, :]
    scale = mod_ref[3 * j + 1:3 * j + 2, :]
    return _rms(x, gain) * (1.0 + scale) + shift


def _mm_in_kernel(*refs, has_small):
    if has_small:
        x_ref, mod_ref, g_ref, w_ref, ws_ref, o_ref, os_ref, h_ref = refs
    else:
        x_ref, mod_ref, g_ref, w_ref, o_ref, h_ref = refs

    j = pl.program_id(1)

    @pl.when(j == 0)
    def _():
        for r in _row_halves(x_ref.shape[0]):
            h_ref[r, :] = _modnorm(x_ref[r, :], mod_ref, g_ref[...], 0).astype(BF16)
            o_ref[r, :] = _dot(h_ref[r, :], w_ref[...]).astype(o_ref.dtype)
            if has_small:
                os_ref[r, :] = _dot(h_ref[r, :], ws_ref[...])

    @pl.when(j > 0)
    def _():
        o_ref[...] = _dot(h_ref[...], w_ref[...]).astype(o_ref.dtype)


def _mm_in(x, mod, gain, w, w_small=None, tm=1024, tn=1152):
    b, l, d = x.shape
    n = w.shape[1]
    m = b * l
    nb = mod.shape[0]
    tm = min(tm, l if nb > 1 else m)
    tiles_per_batch = l // tm
    mod_idx = (lambda i, j: (i // tiles_per_batch, 0, 0)) if nb > 1 else (lambda i, j: (0, 0, 0))
    has_small = w_small is not None
    in_specs = [
        pl.BlockSpec((tm, d), lambda i, j: (i, 0)),
        pl.BlockSpec((None, MOD_COUNT, d), mod_idx),
        pl.BlockSpec((1, d), lambda i, j: (0, 0)),
        pl.BlockSpec((d, tn), lambda i, j: (0, j)),
    ]
    args = [x.reshape(m, d), mod, gain, w]
    out_specs = [pl.BlockSpec((tm, tn), lambda i, j: (i, j))]
    out_shape = [jax.ShapeDtypeStruct((m, n), BF16)]
    if has_small:
        ns = w_small.shape[1]
        in_specs.append(pl.BlockSpec((d, ns), lambda i, j: (0, 0)))
        args.append(w_small)
        out_specs.append(pl.BlockSpec((tm, ns), lambda i, j: (i, 0)))
        out_shape.append(jax.ShapeDtypeStruct((m, ns), F32))
    res = pl.pallas_call(
        functools.partial(_mm_in_kernel, has_small=has_small),
        grid=(m // tm, n // tn),
        in_specs=in_specs,
        out_specs=out_specs,
        out_shape=out_shape,
        scratch_shapes=[pltpu.VMEM((tm, d), BF16)],
        compiler_params=_cparams(("arbitrary", "arbitrary")),
        name="mm_in",
    )(*args)
    if has_small:
        return res[0].reshape(b, l, n), res[1].reshape(b, l, -1)
    return res[0].reshape(b, l, n)


def _mm_out_kernel(*refs, n_lhs):
    lhs_refs = refs[:n_lhs]
    w_ref, x_ref, mod_ref, g_ref, o_ref = refs[n_lhs:]
    gate = mod_ref[2:3, :]
    for r in _row_halves(x_ref.shape[0]):
        acc = None
        k0 = 0
        for a_ref in lhs_refs:
            kk = a_ref.shape[1]
            part = _dot(a_ref[r, :].astype(BF16), w_ref[k0:k0 + kk, :])
            acc = part if acc is None else acc + part
            k0 += kk
        o_ref[r, :] = x_ref[r, :] + gate * _rms(acc, g_ref[...])


def _mm_out(lhs_list, w, x, mod, gain, tm=512):
    b, l, d = x.shape
    m = b * l
    nb = mod.shape[0]
    tm = min(tm, l if nb > 1 else m)
    tiles_per_batch = l // tm
    mod_idx = (lambda i: (i // tiles_per_batch, 0, 0)) if nb > 1 else (lambda i: (0, 0, 0))
    ktot = w.shape[0]
    in_specs = [pl.BlockSpec((tm, a.shape[-1]), lambda i: (i, 0)) for a in lhs_list]
    in_specs += [
        pl.BlockSpec((ktot, d), lambda i: (0, 0), pipeline_mode=pl.Buffered(1)),
        pl.BlockSpec((tm, d), lambda i: (i, 0)),
        pl.BlockSpec((None, MOD_COUNT, d), mod_idx),
        pl.BlockSpec((1, d), lambda i: (0, 0)),
    ]
    out = pl.pallas_call(
        functools.partial(_mm_out_kernel, n_lhs=len(lhs_list)),
        grid=(m // tm,),
        in_specs=in_specs,
        out_specs=pl.BlockSpec((tm, d), lambda i: (i, 0)),
        out_shape=jax.ShapeDtypeStruct((m, d), F32),
        compiler_params=_cparams(("arbitrary",)),
        name="mm_out",
    )(*[a.reshape(m, a.shape[-1]) for a in lhs_list], w, x.reshape(m, d), mod, gain)
    return out.reshape(b, l, d)


def _ffn_kernel(x_ref, mod_ref, g_ref, wg_ref, wu_ref, wd_ref, o_ref, h_ref, acc_ref):
    f = pl.program_id(1)
    nf = pl.num_programs(1)
    halves = _row_halves(x_ref.shape[0])

    def down(rows):
        h = h_ref[rows, :]
        a = _silu(_dot(h, wg_ref[...])) * _dot(h, wu_ref[...])
        return _dot(a.astype(BF16), wd_ref[...])

    @pl.when(f == 0)
    def _():
        for r in halves:
            h_ref[r, :] = _modnorm(x_ref[r, :], mod_ref, g_ref[0:1, :], 1).astype(BF16)
            acc_ref[r, :] = down(r)

    @pl.when((f > 0) & (f < nf - 1))
    def _():
        acc_ref[...] += down(slice(None))

    @pl.when(f == nf - 1)
    def _():
        gate = mod_ref[5:6, :]
        for r in halves:
            o_ref[r, :] = x_ref[r, :] + gate * _rms(acc_ref[r, :] + down(r), g_ref[1:2, :])


def _ffn(x, mod, gains, wg, wu, wd, tm=512, tf=512):
    b, l, d = x.shape
    m = b * l
    ff = wg.shape[1]
    nb = mod.shape[0]
    tm = min(tm, l if nb > 1 else m)
    tiles_per_batch = l // tm
    mod_idx = (lambda i, f: (i // tiles_per_batch, 0, 0)) if nb > 1 else (lambda i, f: (0, 0, 0))
    out = pl.pallas_call(
        _ffn_kernel,
        grid=(m // tm, ff // tf),
        in_specs=[
            pl.BlockSpec((tm, d), lambda i, f: (i, 0)),
            pl.BlockSpec((None, MOD_COUNT, d), mod_idx),
            pl.BlockSpec((2, d), lambda i, f: (0, 0)),
            pl.BlockSpec((d, tf), lambda i, f: (0, f)),
            pl.BlockSpec((d, tf), lambda i, f: (0, f)),
            pl.BlockSpec((tf, d), lambda i, f: (f, 0)),
        ],
        out_specs=pl.BlockSpec((tm, d), lambda i, f: (i, 0)),
        out_shape=jax.ShapeDtypeStruct((m, d), F32),
        scratch_shapes=[pltpu.VMEM((tm, d), BF16), pltpu.VMEM((tm, d), F32)],
        compiler_params=_cparams(("arbitrary", "arbitrary")),
        name="ffn",
    )(x.reshape(m, d), mod, gains, wg, wu, wd)
    return out.reshape(b, l, d)


def _rope_rotate(x, cos, sin):
    half = x.shape[1] // 2
    swapped = jnp.concatenate(
        [pltpu.roll(x[:, :half], half // 2, axis=1), pltpu.roll(x[:, half:], half // 2, axis=1)], axis=1)
    return x * cos + swapped * sin


def _gla_kernel(*refs, seq, rope, has_s0, want_state):
    it = iter(refs)
    q_ref, k_ref, v_ref, g_ref, lr_ref, wup_ref, bup_ref, gn_ref = [next(it) for _ in range(8)]
    cos_ref = sin_ref = s0_ref = snew_ref = None
    if rope:
        cos_ref, sin_ref = next(it), next(it)
    if has_s0:
        s0_ref = next(it)
    o_ref = next(it)
    if want_state:
        snew_ref = next(it)
    s_ref, oacc_ref, qd_ref, k2_ref, ec_ref = [next(it) for _ in range(5)]

    cs = GLA_CHUNK
    nc = seq // cs
    _, dk, dv = s_ref.shape
    grp = GLA_PREPASS_CHUNKS
    rg = grp * cs
    ri = lax.broadcasted_iota(jnp.int32, (rg, rg), 0)
    ci = lax.broadcasted_iota(jnp.int32, (rg, rg), 1)
    same_chunk = (ri // cs) == (ci // cs)
    masks = (same_chunk & (ri >= ci), same_chunk & (ri <= ci))
    masks_bf = tuple(jnp.where(m, 1.0, 0.0).astype(BF16) for m in masks)

    def prepass(gi, carry):
        rows = pl.ds(pl.multiple_of(gi * rg, rg), rg)
        q = q_ref[rows, :].astype(F32) * (dk ** -0.5)
        k = k_ref[rows, :].astype(F32)
        if rope:
            cos, sin = cos_ref[rows, :], sin_ref[rows, :]
            q = _rope_rotate(q, cos, sin)
            k = _rope_rotate(k, cos, sin)
        v = v_ref[rows, :]
        lr = lr_ref[rows, :]
        intra = None
        for dirn in range(2):
            la = _log_sigmoid(_dot(lr, wup_ref[dirn]) + bup_ref[dirn]) * (LOG2_E / GLA_TAU)
            b = _dot_exact_rhs(masks_bf[dirn], la)
            last = cs - 1 if dirn == 0 else 0
            totals = [b[j * cs + last:j * cs + last + 1, :] for j in range(grp)]
            b_all = jnp.concatenate([jnp.broadcast_to(t, (cs, dk)) for t in totals], axis=0)
            qd = (q * jnp.exp2(b)).astype(BF16)
            kd = (k * jnp.exp2(-b)).astype(BF16)
            qd_ref[dirn, rows, :] = qd
            k2_ref[dirn, rows, :] = (k * jnp.exp2(b_all - b)).astype(BF16)
            for j in range(grp):
                ec_ref[dirn, gi * grp + j] = _row_to_col(jnp.exp2(totals[j]))
            sc = jnp.where(masks[dirn], _dot_nt(qd, kd), 0.0)
            part = _dot(sc.astype(BF16), v)
            intra = part if intra is None else intra + part
        oacc_ref[rows, :] = intra
        return carry

    lax.fori_loop(0, seq // rg, prepass, 0, unroll=2 if seq // rg > 1 else 1)

    def step(c, dirn):
        rows = pl.ds(pl.multiple_of(c * cs, cs), cs)
        s_old = s_ref[dirn]
        inter = _dot(qd_ref[dirn, rows, :], s_old.astype(BF16))
        decay = jnp.concatenate([ec_ref[dirn, c]] * (dv // LANES), axis=1)
        s_ref[dirn] = decay * s_old + _dot_tn(k2_ref[dirn, rows, :], v_ref[rows, :])
        return rows, inter

    for dirn in range(2):
        if has_s0:
            s_ref[dirn] = s0_ref[dirn]
        else:
            s_ref[dirn] = jnp.zeros((dk, dv), F32)

    def first_half(i, carry):
        for dirn, c in ((0, i), (1, nc - 1 - i)):
            rows, inter = step(c, dirn)
            oacc_ref[rows, :] += inter
        return carry

    def second_half(i, carry):
        for dirn, c in ((0, i), (1, nc - 1 - i)):
            rows, inter = step(c, dirn)
            o = _rms(oacc_ref[rows, :] + inter, gn_ref[...]) * _silu(g_ref[rows, :].astype(F32))
            o_ref[rows, :] = o.astype(o_ref.dtype)
        return carry

    half_unroll = 2 if nc % 4 == 0 else 1
    lax.fori_loop(0, nc // 2, first_half, 0, unroll=half_unroll)
    lax.fori_loop(nc // 2, nc, second_half, 0, unroll=half_unroll)
    if want_state:
        for dirn in range(2):
            snew_ref[dirn] = s_ref[dirn]


def _gla(proj, col, wup, bup, gn, rope_tabs, s0, want_state):
    b, l, _ = proj.shape
    h = GLA_HEADS
    dk = wup.shape[2] // h
    dv = gn.shape[1] // h
    nc = l // GLA_CHUNK
    rope = rope_tabs is not None
    has_s0 = s0 is not None
    in_specs = [
        pl.BlockSpec((None, l, dk), lambda bi, hi: (bi, 0, col["q"] // dk + hi)),
        pl.BlockSpec((None, l, dk), lambda bi, hi: (bi, 0, col["k"] // dk + hi)),
        pl.BlockSpec((None, l, dv), lambda bi, hi: (bi, 0, col["v"] // dv + hi)),
        pl.BlockSpec((None, l, dv), lambda bi, hi: (bi, 0, col["g"] // dv + hi)),
        pl.BlockSpec((None, l, LANES), lambda bi, hi: (bi, 0, col["lr"] // LANES)),
        pl.BlockSpec((2, LANES, dk), lambda bi, hi: (0, 0, hi)),
        pl.BlockSpec((2, 1, dk), lambda bi, hi: (0, 0, hi)),
        pl.BlockSpec((1, dv), lambda bi, hi: (0, hi)),
    ]
    args = [proj, proj, proj, proj, proj, wup, bup, gn]
    if rope:
        in_specs += [pl.BlockSpec((l, dk), lambda bi, hi: (0, 0), pipeline_mode=pl.Buffered(1))] * 2
        args += list(rope_tabs)
    if has_s0:
        in_specs.append(pl.BlockSpec((None, 2, None, dk, dv), lambda bi, hi: (bi, 0, hi, 0, 0)))
        args.append(s0)
    out_specs = [pl.BlockSpec((None, l, dv), lambda bi, hi: (bi, 0, hi))]
    out_shape = [jax.ShapeDtypeStruct((b, l, h * dv), BF16)]
    if want_state:
        out_specs.append(pl.BlockSpec((None, 2, None, dk, dv), lambda bi, hi: (bi, 0, hi, 0, 0)))
        out_shape.append(jax.ShapeDtypeStruct((b, 2, h, dk, dv), F32))
    res = pl.pallas_call(
        functools.partial(_gla_kernel, seq=l, rope=rope, has_s0=has_s0, want_state=want_state),
        grid=(b, h),
        in_specs=in_specs,
        out_specs=out_specs,
        out_shape=out_shape,
        scratch_shapes=[
            pltpu.VMEM((2, dk, dv), F32), pltpu.VMEM((l, dv), F32),
            pltpu.VMEM((2, l, dk), BF16), pltpu.VMEM((2, l, dk), BF16),
            pltpu.VMEM((2, nc, dk, LANES), F32),
        ],
        compiler_params=_cparams(("arbitrary", "arbitrary")),
        name="gla",
    )(*args)
    return res if want_state else res[0]


def _tile_scan(a, u, reverse):
    n = a.shape[0]
    rid = lax.broadcasted_iota(jnp.int32, a.shape, 0)
    for d in (1, 2, 4):
        shift, m = (n - d, rid < n - d) if reverse else (d, rid >= d)
        a_s = pltpu.roll(a, shift, axis=0)
        u_s = pltpu.roll(u, shift, axis=0)
        u = jnp.where(m, u + a * u_s, u)
        a = jnp.where(m, a * a_s, a)
    return a, u


def _conv4(pad_ref, r0, n, halo, w_ref, b_ref):
    xw = pad_ref[pl.ds(r0, n + 2 * halo), :]
    tot = n + 2 * halo
    return (w_ref[0:1, :] * pltpu.roll(xw, 2, axis=0)[halo:halo + n]
            + w_ref[1:2, :] * pltpu.roll(xw, 1, axis=0)[halo:halo + n]
            + w_ref[2:3, :] * xw[halo:halo + n]
            + w_ref[3:4, :] * pltpu.roll(xw, tot - 1, axis=0)[halo:halo + n]
            + b_ref[...])


def _fill_padded(pad_ref, src_ref, seq, halo, step):
    wd = pad_ref.shape[1]
    pad_ref[0:halo, :] = jnp.zeros((halo, wd), F32)
    pad_ref[seq + halo:seq + 2 * halo, :] = jnp.zeros((halo, wd), F32)

    def body(i, carry):
        r0 = pl.multiple_of(i * step, step)
        pad_ref[pl.ds(r0 + halo, step), :] = src_ref[pl.ds(r0, step), :].astype(F32)
        return carry

    lax.fori_loop(0, seq // step, body, 0)


def _rglru_kernel(*refs, seq, has_h0, want_state, tc):
    it = iter(refs)
    xr_ref, yr_ref, cw_ref, cb_ref, wg_ref, bg_ref, lam_ref = [next(it) for _ in range(7)]
    h0_ref = next(it) if has_h0 else None
    y_ref = next(it)
    hl_ref = next(it) if want_state else None
    xpad_ref, af_ref, uf_ref, ab_ref, ub_ref, gy_ref = [next(it) for _ in range(6)]

    wb = xr_ref.shape[1]
    nblk = wb // RNN_BLOCK
    halo = 8
    _fill_padded(xpad_ref, xr_ref, seq, halo, tc)
    ls = _log_sigmoid(lam_ref[...])

    def gates(c, carry):
        r0 = pl.multiple_of(c * tc, tc)
        rows = pl.ds(r0, tc)
        xc = _conv4(xpad_ref, r0, tc, halo, cw_ref, cb_ref)
        gy_ref[rows, :] = _gelu_tanh(yr_ref[rows, :].astype(F32))
        for j in range(nblk):
            lanes = slice(j * RNN_BLOCK, (j + 1) * RNN_BLOCK)
            xcj = xc[:, lanes]
            gt = _dot(xcj.astype(BF16), wg_ref[j]) + bg_ref[j]
            for dirn, (a_ref, u_ref) in enumerate(((af_ref, uf_ref), (ab_ref, ub_ref))):
                r = _sigmoid(gt[:, dirn * RNN_BLOCK:(dirn + 1) * RNN_BLOCK])
                i = _sigmoid(gt[:, (2 + dirn) * RNN_BLOCK:(3 + dirn) * RNN_BLOCK])
                log_a = RNN_C * r * ls[dirn:dirn + 1, lanes]
                a = jnp.exp(log_a)
                u = jnp.sqrt(-jnp.tanh(log_a) * (a * a + 1.0)) * i * xcj
                a_ref[rows, lanes] = a
                u_ref[rows, lanes] = u
        return carry

    lax.fori_loop(0, seq // tc, gates, 0)

    h0f = h0_ref[0:1, :] if has_h0 else jnp.zeros((1, wb), F32)
    h0b = h0_ref[1:2, :] if has_h0 else jnp.zeros((1, wb), F32)

    def fwd(t, hc):
        rows = pl.ds(pl.multiple_of(t * 8, 8), 8)
        a, u = _tile_scan(af_ref[rows, :], uf_ref[rows, :], False)
        hh = u + a * hc
        uf_ref[rows, :] = hh
        return hh[7:8, :]

    hlf = lax.fori_loop(0, seq // 8, fwd, h0f, unroll=4)

    n16 = seq // 16

    def bwd(i, hc):
        r16 = pl.multiple_of((n16 - 1 - i) * 16, 16)
        lo, hi = pl.ds(r16, 8), pl.ds(r16 + 8, 8)
        a1, u1 = _tile_scan(ab_ref[hi, :], ub_ref[hi, :], True)
        h1 = u1 + a1 * hc
        a0, u0 = _tile_scan(ab_ref[lo, :], ub_ref[lo, :], True)
        h0 = u0 + a0 * h1[0:1, :]
        y = jnp.concatenate([(uf_ref[lo, :] + h0) * gy_ref[lo, :], (uf_ref[hi, :] + h1) * gy_ref[hi, :]], axis=0)
        y_ref[pl.ds(r16, 16), :] = y.astype(y_ref.dtype)
        return h0[0:1, :]

    hlb = lax.fori_loop(0, n16, bwd, h0b, unroll=2)
    if want_state:
        hl_ref[0:1, :] = hlf
        hl_ref[1:2, :] = hlb


def _rglru(proj, col, cw, cb, wg, bg, lam, h0, want_state, wb=512, tc=256):
    b, l, _ = proj.shape
    w = cw.shape[1]
    tc = min(tc, l)
    nblk = wb // RNN_BLOCK
    has_h0 = h0 is not None
    in_specs = [
        pl.BlockSpec((None, l, wb), lambda bi, j: (bi, 0, col["xr"] // wb + j)),
        pl.BlockSpec((None, l, wb), lambda bi, j: (bi, 0, col["yr"] // wb + j)),
        pl.BlockSpec((4, wb), lambda bi, j: (0, j)),
        pl.BlockSpec((1, wb), lambda bi, j: (0, j)),
        pl.BlockSpec((nblk, RNN_BLOCK, 4 * RNN_BLOCK), lambda bi, j: (j, 0, 0)),
        pl.BlockSpec((nblk, 1, 4 * RNN_BLOCK), lambda bi, j: (j, 0, 0)),
        pl.BlockSpec((2, wb), lambda bi, j: (0, j)),
    ]
    args = [proj, proj, cw, cb, wg, bg, lam]
    if has_h0:
        in_specs.append(pl.BlockSpec((None, 2, wb), lambda bi, j: (bi, 0, j)))
        args.append(h0)
    out_specs = [pl.BlockSpec((None, l, wb), lambda bi, j: (bi, 0, j))]
    out_shape = [jax.ShapeDtypeStruct((b, l, w), BF16)]
    if want_state:
        out_specs.append(pl.BlockSpec((None, 2, wb), lambda bi, j: (bi, 0, j)))
        out_shape.append(jax.ShapeDtypeStruct((b, 2, w), F32))
    res = pl.pallas_call(
        functools.partial(_rglru_kernel, seq=l, has_h0=has_h0, want_state=want_state, tc=tc),
        grid=(b, w // wb),
        in_specs=in_specs,
        out_specs=out_specs,
        out_shape=out_shape,
        scratch_shapes=[pltpu.VMEM((l + 16, wb), F32)] + [pltpu.VMEM((l, wb), F32)] * 5,
        compiler_params=_cparams(("arbitrary", "arbitrary")),
        name="rglru",
    )(*args)
    return res if want_state else res[0]


def _ssd_dt_kernel(dt_ref, bias_ref, alog_ref, cf_ref, cb_ref, qf_ref, qb_ref, cfr_ref, cbr_ref, qfr_ref, qbr_ref,
                   *, seq):
    cs = SSD_CHUNK
    tril, triu = _tri_masks(cs)
    tril_bf = jnp.where(tril, 1.0, 0.0).astype(BF16)
    triu_bf = jnp.where(triu, 1.0, 0.0).astype(BF16)
    a = -jnp.exp(alog_ref[...])

    def body(c, carry):
        rows = pl.ds(pl.multiple_of(c * cs, cs), cs)
        dt = _softplus(dt_ref[rows, :] + bias_ref[...])
        la = dt * a
        log_dt = jnp.log(dt)
        cf = _dot_exact_rhs(tril_bf, la)
        cb = _dot_exact_rhs(triu_bf, la)
        for val, col_ref, row_ref in ((cf, cf_ref, cfr_ref), (cb, cb_ref, cbr_ref),
                                      (cf - log_dt, qf_ref, qfr_ref), (cb - log_dt, qb_ref, qbr_ref)):
            val = val * LOG2_E
            col_ref[rows, :] = val
            row_ref[c] = val.T
        return carry

    lax.fori_loop(0, seq // cs, body, 0)


def _ssd_dt(dt_raw, bias, alog):
    b, l, _ = dt_raw.shape
    nc = l // SSD_CHUNK
    cshp = jax.ShapeDtypeStruct((b, l, LANES), F32)
    cspec = pl.BlockSpec((None, l, LANES), lambda bi: (bi, 0, 0))
    rshp = jax.ShapeDtypeStruct((b, nc, LANES, SSD_CHUNK), F32)
    rspec = pl.BlockSpec((None, nc, LANES, SSD_CHUNK), lambda bi: (bi, 0, 0, 0))
    return pl.pallas_call(
        functools.partial(_ssd_dt_kernel, seq=l),
        grid=(b,),
        in_specs=[
            pl.BlockSpec((None, l, LANES), lambda bi: (bi, 0, 0)),
            pl.BlockSpec((1, LANES), lambda bi: (0, 0)),
            pl.BlockSpec((1, LANES), lambda bi: (0, 0)),
        ],
        out_specs=[cspec] * 4 + [rspec] * 4,
        out_shape=[cshp] * 4 + [rshp] * 4,
        compiler_params=_cparams(("arbitrary",)),
        name="ssd_dt",
    )(dt_raw, bias, alog)


def _ssd_kernel(*refs, seq, has_s0, want_state):
    it = iter(refs)
    (z_ref, x_ref, bm_ref, cm_ref, cfc_ref, cbc_ref, qfc_ref, qbc_ref, cfr_ref, cbr_ref, qfr_ref, qbr_ref,
     cwx_ref, cwb_ref, cwc_ref, cbx_ref, cbb_ref, cbc2_ref, dsk_ref, gn_ref) = [next(it) for _ in range(20)]
    s0_ref = next(it) if has_s0 else None
    y_ref = next(it)
    snew_ref = next(it) if want_state else None
    (xpad_ref, bpad_ref, cpad_ref, xs_ref, bt_ref, cs_ref, cbm_ref, yacc_ref, st_ref) = [next(it) for _ in range(9)]
    cum_cols, cum_rows = (cfc_ref, cbc_ref), (cfr_ref, cbr_ref)
    q_cols, q_rows = (qfc_ref, qbc_ref), (qfr_ref, qbr_ref)

    cs = SSD_CHUNK
    nc = seq // cs
    hp = SSD_HEAD_DIM
    gw = x_ref.shape[1]
    hg = gw // hp
    npair = gw // LANES
    halo = 8
    _fill_padded(xpad_ref, x_ref, seq, halo, cs)
    _fill_padded(bpad_ref, bm_ref, seq, halo, cs)
    _fill_padded(cpad_ref, cm_ref, seq, halo, cs)

    def conv(c, carry):
        r0 = pl.multiple_of(c * cs, cs)
        rows = pl.ds(r0, cs)
        xs_ref[rows, :] = _silu(_conv4(xpad_ref, r0, cs, halo, cwx_ref, cbx_ref))
        bc = _silu(_conv4(bpad_ref, r0, cs, halo, cwb_ref, cbb_ref))
        cc = _silu(_conv4(cpad_ref, r0, cs, halo, cwc_ref, cbc2_ref)).astype(BF16)
        bt_ref[c] = bc.T.astype(BF16)
        cs_ref[rows, :] = cc
        cbm_ref[rows, :] = _dot_nt(cc, bc.astype(BF16))
        return carry

    lax.fori_loop(0, nc, conv, 0, unroll=2)

    masks = _tri_masks(cs)
    lane = lax.broadcasted_iota(jnp.int32, (cs, LANES), 1)
    first_of_pair = lane < hp
    gbase = pl.program_id(1) * (2 * hg)

    def chunk(c, dirn):
        rows = pl.ds(pl.multiple_of(c * cs, cs), cs)
        xs = xs_ref[rows, :]
        c_c = cs_ref[rows, :]
        cbm = cbm_ref[rows, :]
        cum_t = cum_cols[dirn][rows, :]
        q_t = q_cols[dirn][rows, :]
        q_r = q_rows[dirn][c]
        last = cs - 1 if dirn == 0 else 0
        end_lane = cum_t[last:last + 1, :]
        w_all = jnp.exp2(end_lane - q_t)
        e_end = jnp.exp2(jnp.broadcast_to(end_lane, (8, LANES)))
        s_old = st_ref[dirn]
        y_state = _dot(c_c, s_old.astype(BF16))
        tiles, xw_tiles, end_tiles = [], [], []
        for m in range(npair):
            h0 = dirn * hg + 2 * m
            idx = gbase + h0 + lane // hp
            xp = xs[:, m * LANES:(m + 1) * LANES]
            xw_tiles.append((xp * jnp.take_along_axis(w_all, idx, axis=1)).astype(BF16))
            end_tiles.append(jnp.take_along_axis(e_end, idx[0:8], axis=1)[0:1])
            rhs = jnp.concatenate([jnp.where(first_of_pair, xp, 0.0), jnp.where(first_of_pair, 0.0, xp)],
                                  axis=0).astype(BF16)
            lhs, ecum = [], []
            for jj in range(2):
                col = jnp.take_along_axis(cum_t, jnp.zeros_like(lane) + (gbase + h0 + jj), axis=1)
                seg = col - q_r[h0 + jj:h0 + jj + 1, :]
                lhs.append((cbm * jnp.exp2(jnp.where(masks[dirn], seg, -jnp.inf))).astype(BF16))
                ecum.append(jnp.exp2(col))
            y_p = _dot(jnp.concatenate(lhs, axis=1), rhs)
            y_p = y_p + y_state[:, m * LANES:(m + 1) * LANES] * jnp.where(first_of_pair, ecum[0], ecum[1])
            tiles.append(y_p)
        xw = jnp.concatenate(xw_tiles, axis=1)
        st_ref[dirn] = s_old * jnp.concatenate(end_tiles, axis=1) + _dot(bt_ref[c], xw)
        return rows, jnp.concatenate(tiles, axis=1)

    for dirn in range(2):
        if has_s0:
            st_ref[dirn] = s0_ref[dirn].T
        else:
            st_ref[dirn] = jnp.zeros(st_ref.shape[1:], F32)

    def finish(rows, y):
        y = y + dsk_ref[...] * xs_ref[rows, :]
        y = y * _silu(z_ref[rows, :].astype(F32))
        y_ref[rows, :] = _rms(y, gn_ref[...]).astype(y_ref.dtype)

    def first_half(i, carry):
        for dirn, c in ((0, i), (1, nc - 1 - i)):
            rows, y = chunk(c, dirn)
            yacc_ref[rows, :] = y
        return carry

    def second_half(i, carry):
        for dirn, c in ((0, i), (1, nc - 1 - i)):
            rows, y = chunk(c, dirn)
            finish(rows, yacc_ref[rows, :] + y)
        return carry

    half_unroll = 2 if nc % 4 == 0 else 1
    lax.fori_loop(0, nc // 2, first_half, 0, unroll=half_unroll)
    lax.fori_loop(nc // 2, nc, second_half, 0, unroll=half_unroll)
    if want_state:
        for dirn in range(2):
            snew_ref[dirn] = st_ref[dirn].T


def _ssd(proj, col, cums, cw, cb, dsk, gn, s0, want_state):
    b, l, _ = proj.shape
    g = SSD_GROUPS
    inner = dsk.shape[1]
    gw = inner // g
    ns = SSD_STATE
    nc = l // SSD_CHUNK
    hg2 = 2 * gw // SSD_HEAD_DIM
    has_s0 = s0 is not None
    bcol0 = inner // ns
    ccol0 = bcol0 + g
    assert nc % 2 == 0
    colspec = pl.BlockSpec((None, l, LANES), lambda bi, gi: (bi, 0, 0))
    rowspec = pl.BlockSpec((None, nc, hg2, SSD_CHUNK), lambda bi, gi: (bi, 0, gi, 0))
    in_specs = [
        pl.BlockSpec((None, l, gw), lambda bi, gi: (bi, 0, col["z"] // gw + gi)),
        pl.BlockSpec((None, l, gw), lambda bi, gi: (bi, 0, col["x"] // gw + gi)),
        pl.BlockSpec((None, l, ns), lambda bi, gi: (bi, 0, col["B"] // ns + gi)),
        pl.BlockSpec((None, l, ns), lambda bi, gi: (bi, 0, col["C"] // ns + gi)),
        colspec, colspec, colspec, colspec, rowspec, rowspec, rowspec, rowspec,
        pl.BlockSpec((4, gw), lambda bi, gi: (0, gi)),
        pl.BlockSpec((4, ns), lambda bi, gi: (0, bcol0 + gi)),
        pl.BlockSpec((4, ns), lambda bi, gi: (0, ccol0 + gi)),
        pl.BlockSpec((1, gw), lambda bi, gi: (0, gi)),
        pl.BlockSpec((1, ns), lambda bi, gi: (0, bcol0 + gi)),
        pl.BlockSpec((1, ns), lambda bi, gi: (0, ccol0 + gi)),
        pl.BlockSpec((1, gw), lambda bi, gi: (0, gi)),
        pl.BlockSpec((1, gw), lambda bi, gi: (0, gi)),
    ]
    args = [proj, proj, proj, proj, *cums, cw, cw, cw, cb, cb, cb, dsk, gn]
    if has_s0:
        in_specs.append(pl.BlockSpec((None, 2, gw, ns), lambda bi, gi: (bi, 0, gi, 0)))
        args.append(s0)
    out_specs = [pl.BlockSpec((None, l, gw), lambda bi, gi: (bi, 0, gi))]
    out_shape = [jax.ShapeDtypeStruct((b, l, inner), BF16)]
    if want_state:
        out_specs.append(pl.BlockSpec((None, 2, gw, ns), lambda bi, gi: (bi, 0, gi, 0)))
        out_shape.append(jax.ShapeDtypeStruct((b, 2, inner, ns), F32))
    res = pl.pallas_call(
        functools.partial(_ssd_kernel, seq=l, has_s0=has_s0, want_state=want_state),
        grid=(b, g),
        in_specs=in_specs,
        out_specs=out_specs,
        out_shape=out_shape,
        scratch_shapes=[
            pltpu.VMEM((l + 16, gw), F32), pltpu.VMEM((l + 16, ns), F32), pltpu.VMEM((l + 16, ns), F32),
            pltpu.VMEM((l, gw), F32), pltpu.VMEM((nc, ns, SSD_CHUNK), BF16), pltpu.VMEM((l, ns), BF16),
            pltpu.VMEM((l, SSD_CHUNK), F32), pltpu.VMEM((l, gw), F32), pltpu.VMEM((2, ns, gw), F32),
        ],
        compiler_params=_cparams(("arbitrary", "arbitrary")),
        name="ssd",
    )(*args)
    return res if want_state else res[0]


def _prep_even(w_in, w_out, w_up, b_up, w_r, b_r, w_i, b_i):
    d = w_in.shape[0]
    qk = w_up.shape[2]
    gv = (w_in.shape[1] - 2 * qk - 2 * GLA_RANK) // 4
    sizes = (qk, qk, gv, gv, 2 * GLA_RANK, gv, gv)
    offs = [0]
    for s in sizes:
        offs.append(offs[-1] + s)
    pad = jnp.zeros((d, LANES - 2 * GLA_RANK), BF16)
    w = jnp.concatenate([w_in[:, :offs[4]].astype(BF16), w_in[:, offs[5]:].astype(BF16),
                         w_in[:, offs[4]:offs[5]].astype(BF16), pad], axis=1)
    col = {"q": 0, "k": qk, "v": 2 * qk, "g": 2 * qk + gv, "xr": 2 * qk + 2 * gv, "yr": 2 * qk + 3 * gv,
           "lr": 2 * qk + 4 * gv}
    wup = jnp.zeros((2, LANES, qk), F32)
    wup = wup.at[0, 0:GLA_RANK].set(w_up[0]).at[1, GLA_RANK:2 * GLA_RANK].set(w_up[1])
    nblk = w_r.shape[1]
    wg = jnp.concatenate([w_r[0], w_r[1], w_i[0], w_i[1]], axis=-1).astype(BF16)
    bg = jnp.concatenate([b.reshape(nblk, 1, RNN_BLOCK) for b in (b_r[0], b_r[1], b_i[0], b_i[1])], axis=-1)
    return {
        "w_in": w, "col": col, "w_out": w_out.astype(BF16),
        "wup": wup.astype(BF16), "bup": b_up.reshape(2, 1, qk), "wg": wg, "bg": bg,
    }


def _group_major(a, heads):
    g = SSD_GROUPS
    hg = heads // g
    lead = a.shape[:-1]
    return jnp.swapaxes(a.reshape(lead + (2, g, hg)), -3, -2).reshape(lead + (2 * heads,))


def _prep_odd(w_in, w_out, dt_bias, a_log, d_skip):
    heads = d_skip.shape[0]
    inner = heads * SSD_HEAD_DIM
    xbc = inner + 2 * SSD_GROUPS * SSD_STATE
    dtw = _group_major(w_in[:, inner + xbc:], heads).astype(BF16)
    col = {"z": 0, "x": inner, "B": 2 * inner, "C": 2 * inner + SSD_GROUPS * SSD_STATE}
    return {
        "w_in": w_in[:, :inner + xbc].astype(BF16), "w_dt": dtw, "col": col, "w_out": w_out.astype(BF16),
        "dt_bias": _group_major(dt_bias.reshape(1, 2 * heads), heads),
        "a_log": _group_major(a_log.reshape(1, 2 * heads), heads),
        "dsk": jnp.repeat(d_skip, SSD_HEAD_DIM).reshape(1, inner),
    }


def _rope_tables(seq, dk):
    nf = dk // 4
    inv = ROPE_BASE ** (-jnp.arange(nf, dtype=F32) / nf)
    t = jnp.arange(seq)
    ang_r = (t // GRID_W).astype(F32)[:, None] * inv
    ang_c = (t % GRID_W).astype(F32)[:, None] * inv
    cos = jnp.concatenate([jnp.cos(ang_r)] * 2 + [jnp.cos(ang_c)] * 2, axis=1)
    sin = jnp.concatenate([-jnp.sin(ang_r), jnp.sin(ang_r), -jnp.sin(ang_c), jnp.sin(ang_c)], axis=1)
    return cos, sin


def _even_mixer(x, mod, gains, p, gla_gn, cw, cb, lam, s0_gla, h0_rnn, rope_tabs, want_state):
    proj = _mm_in(x, mod, gains[0:1], p["w_in"])
    gla = _gla(proj, p["col"], p["wup"], p["bup"], gla_gn, rope_tabs, s0_gla, want_state)
    rnn = _rglru(proj, p["col"], cw, cb, p["wg"], p["bg"], lam, h0_rnn, want_state)
    if want_state:
        (o, sg), (yr, sr) = gla, rnn
    else:
        (o, sg), (yr, sr) = (gla, None), (rnn, None)
    y = _mm_out([o, yr], p["w_out"], x, mod, gains[1:2])
    return y, sg, sr


def _odd_mixer(x, mod, gains, p, cw, cb, gn, s0, want_state):
    proj, dt_raw = _mm_in(x, mod, gains[0:1], p["w_in"], p["w_dt"], tn=1024)
    cums = _ssd_dt(dt_raw, p["dt_bias"], p["a_log"])
    res = _ssd(proj, p["col"], cums, cw, cb, p["dsk"], gn, s0, want_state)
    yv, ss = res if want_state else (res, None)
    y = _mm_out([yv], p["w_out"], x, mod, gains[1:2])
    return y, ss


def kernel(x_prompt, x_sample, state_gla, state_rglru, state_ssd, c, c_ctx, w_ada, b_ada, norm_g, ev_w_in, ev_w_out,
           gla_w_up, gla_b_up, gla_norm_g, rnn_conv_w, rnn_conv_b, rnn_w_r, rnn_b_r, rnn_w_i, rnn_b_i, rnn_lam,
           od_w_in, od_w_out, ssd_conv_w, ssd_conv_b, ssd_dt_bias, ssd_a_log, ssd_d, ssd_norm_g,
           ffn_w_gate, ffn_w_up, ffn_w_down):
    depth, d = norm_g.shape[0], norm_g.shape[2]
    nb = c.shape[0]
    rows = -(-(nb + 1) // 8) * 8
    cvec = jnp.zeros((rows, d), F32).at[0].set(c_ctx).at[1:nb + 1].set(c)
    mods = _adaln(cvec, w_ada, b_ada).reshape(depth, rows, MOD_COUNT, d)
    rope_tabs = _rope_tables(x_sample.shape[1], gla_w_up.shape[3] // GLA_HEADS)

    yp, ys = x_prompt, x_sample
    new_gla, new_rnn, new_ssd = [], [], []
    for l in range(depth):
        mod_p, mod_s = mods[l, 0:1], mods[l, 1:nb + 1]
        gains = norm_g[l]
        if l % 2 == 0:
            e = l // 2
            p = _prep_even(ev_w_in[e], ev_w_out[e], gla_w_up[e], gla_b_up[e], rnn_w_r[e], rnn_b_r[e],
                           rnn_w_i[e], rnn_b_i[e])
            gn = gla_norm_g[e].reshape(1, -1)
            cw, cb, lam = rnn_conv_w[e], rnn_conv_b[e].reshape(1, -1), rnn_lam[e]
            yp, sg, sr = _even_mixer(yp, mod_p, gains, p, gn, cw, cb, lam, None, None, None, True)
            ys, _, _ = _even_mixer(ys, mod_s, gains, p, gn, cw, cb, lam, state_gla[:, e], state_rglru[:, e],
                                   rope_tabs, False)
            new_gla.append(sg)
            new_rnn.append(sr)
        else:
            o = l // 2
            p = _prep_odd(od_w_in[o], od_w_out[o], ssd_dt_bias[o], ssd_a_log[o], ssd_d[o])
            cw, cb, gn = ssd_conv_w[o], ssd_conv_b[o].reshape(1, -1), ssd_norm_g[o].reshape(1, -1)
            inner, ns = gn.shape[1], SSD_STATE
            s0 = state_ssd[:, o].reshape(nb, 2, inner, ns)
            yp, ss = _odd_mixer(yp, mod_p, gains, p, cw, cb, gn, None, True)
            ys, _ = _odd_mixer(ys, mod_s, gains, p, cw, cb, gn, s0, False)
            new_ssd.append(ss.reshape(ss.shape[0], 2, inner // SSD_HEAD_DIM, SSD_HEAD_DIM, ns))
        wg, wu, wd = ffn_w_gate[l].astype(BF16), ffn_w_up[l].astype(BF16), ffn_w_down[l].astype(BF16)
        yp = _ffn(yp, mod_p, gains[2:4], wg, wu, wd)
        ys = _ffn(ys, mod_s, gains[2:4], wg, wu, wd)
    return (yp, ys, jnp.stack(new_gla, axis=1), jnp.stack(new_rnn, axis=1), jnp.stack(new_ssd, axis=1))
```

```python
import functools
import math

import jax
import jax.numpy as jnp
from jax import lax
from jax.experimental import pallas as pl
from jax.experimental.pallas import tpu as pltpu

F32 = jnp.float32
BF16 = jnp.bfloat16

EPS = 1e-6
MOD_COUNT = 6
GLA_HEADS = 4
GLA_RANK = 16
GLA_TAU = 16.0
GLA_CHUNK = 64
GLA_PREPASS_CHUNKS = 4
ROPE_BASE = 10000.0
GRID_W = 64
RNN_BLOCK = 128
RNN_C = 8.0
SSD_HEAD_DIM = 64
SSD_STATE = 128
SSD_GROUPS = 8
SSD_CHUNK = 128

LOG2_E = math.log2(math.e)
LANES = 128
V7X_VMEM_BYTES = 64 * 1024 * 1024
VMEM_LIMIT = V7X_VMEM_BYTES - 8 * 1024 * 1024


def _cparams(sem):
    return pltpu.CompilerParams(dimension_semantics=sem, vmem_limit_bytes=VMEM_LIMIT)


def _sigmoid(x):
    return 1.0 / (1.0 + jnp.exp(-x))


def _silu(x):
    h = 0.5 * x
    return h + h * jnp.tanh(h)


def _softplus(x):
    return jnp.maximum(x, 0.0) + jnp.log1p(jnp.exp(-jnp.abs(x)))


def _log_sigmoid(x):
    return jnp.minimum(x, 0.0) - jnp.log1p(jnp.exp(-jnp.abs(x)))


def _gelu_tanh(x):
    return 0.5 * x * (1.0 + jnp.tanh(math.sqrt(2.0 / math.pi) * (x + 0.044715 * (x * x * x))))


def _rms(x, gain):
    ms = jnp.mean(x * x, axis=-1, keepdims=True)
    return x * lax.rsqrt(ms + EPS) * gain


def _split3(x):
    hi = x.astype(BF16)
    r1 = x - hi.astype(F32)
    mid = r1.astype(BF16)
    lo = (r1 - mid.astype(F32)).astype(BF16)
    return hi, mid, lo


def _dot(a, b):
    return jnp.dot(a, b, preferred_element_type=F32)


def _dot_nt(a, b):
    return lax.dot_general(a, b, (((1,), (1,)), ((), ())), preferred_element_type=F32)


def _dot_tn(a, b):
    return lax.dot_general(a, b, (((0,), (0,)), ((), ())), preferred_element_type=F32)


def _dot_exact_rhs(a_bf16, x):
    hi, mid, lo = _split3(x)
    return _dot(a_bf16, hi) + _dot(a_bf16, mid) + _dot(a_bf16, lo)


def _tri_masks(n):
    ri = lax.broadcasted_iota(jnp.int32, (n, n), 0)
    ci = lax.broadcasted_iota(jnp.int32, (n, n), 1)
    return ri >= ci, ri <= ci


def _row_halves(n):
    return (slice(0, n // 2), slice(n // 2, n))


def _row_to_col(row):
    cols = []
    for j in range(row.shape[1] // LANES):
        t = jnp.broadcast_to(row[:, j * LANES:(j + 1) * LANES], (16, LANES)).T
        cols.append(jnp.broadcast_to(t[:, 0:1], (LANES, LANES)))
    return jnp.concatenate(cols, axis=0)


def _cast_kernel(x_ref, o_ref):
    o_ref[...] = x_ref[...].astype(o_ref.dtype)


def _cast_bf16(w, layer, cols=None, block_bytes=6 * 1024 * 1024):
    _, r, n = w.shape
    cols = n if cols is None else cols
    tr = r
    while tr * cols * 4 > block_bytes and tr % 16 == 0:
        tr //= 2
    return pl.pallas_call(
        _cast_kernel,
        grid=(r // tr,),
        in_specs=[pl.BlockSpec((None, tr, cols), lambda i: (layer, i, 0))],
        out_specs=pl.BlockSpec((tr, cols), lambda i: (i, 0)),
        out_shape=jax.ShapeDtypeStruct((r, cols), BF16),
        compiler_params=_cparams(("arbitrary",)),
        name="cast",
    )(w)


def _even_w_in_kernel(x_ref, o_ref, *, lo, hi):
    n = x_ref.shape[1]
    rows = x_ref.shape[0]
    tail = lo + (n - hi)
    o_ref[:, 0:lo] = x_ref[:, 0:lo].astype(BF16)
    o_ref[:, lo:tail] = x_ref[:, hi:n].astype(BF16)
    lane = lax.broadcasted_iota(jnp.int32, (rows, LANES), 1)
    o_ref[:, tail:tail + LANES] = jnp.where(lane < hi - lo, x_ref[:, lo:lo + LANES], 0.0).astype(BF16)


def _even_w_in(w, layer, lo, hi, tr=256):
    _, d, n = w.shape
    n_out = lo + (n - hi) + LANES
    return pl.pallas_call(
        functools.partial(_even_w_in_kernel, lo=lo, hi=hi),
        grid=(d // tr,),
        in_specs=[pl.BlockSpec((None, tr, n), lambda i: (layer, i, 0))],
        out_specs=pl.BlockSpec((tr, n_out), lambda i: (i, 0)),
        out_shape=jax.ShapeDtypeStruct((d, n_out), BF16),
        compiler_params=_cparams(("arbitrary",)),
        name="even_w_in",
    )(w)


def _adaln_kernel(c_ref, w_ref, b_ref, o_ref):
    s = _silu(c_ref[...]).astype(BF16)
    o_ref[...] = _dot(s, w_ref[...].astype(BF16)) + b_ref[...]


def _adaln(cvec, w_ada, b_ada, tn=1024):
    depth, d, n = w_ada.shape
    r = cvec.shape[0]
    return pl.pallas_call(
        _adaln_kernel,
        grid=(depth, n // tn),
        in_specs=[
            pl.BlockSpec((r, d), lambda l, j: (0, 0)),
            pl.BlockSpec((None, d, tn), lambda l, j: (l, 0, j)),
            pl.BlockSpec((None, 1, tn), lambda l, j: (l, 0, j)),
        ],
        out_specs=pl.BlockSpec((None, r, tn), lambda l, j: (l, 0, j)),
        out_shape=jax.ShapeDtypeStruct((depth, r, n), F32),
        compiler_params=_cparams(("arbitrary", "arbitrary")),
        name="adaln",
    )(cvec, w_ada, b_ada.reshape(depth, 1, n))


def _modnorm(x, mod_ref, gain, j):
    shift = mod_ref[3 * j:3 * j + 1, :]
    scale = mod_ref[3 * j + 1:3 * j + 2, :]
    return _rms(x, gain) * (1.0 + scale) + shift


def _mm_in_kernel(*refs, has_small):
    if has_small:
        x_ref, mod_ref, g_ref, w_ref, ws_ref, o_ref, os_ref, h_ref = refs
    else:
        x_ref, mod_ref, g_ref, w_ref, o_ref, h_ref = refs

    j = pl.program_id(1)

    @pl.when(j == 0)
    def _():
        for r in _row_halves(x_ref.shape[0]):
            h_ref[r, :] = _modnorm(x_ref[r, :], mod_ref, g_ref[...], 0).astype(BF16)
            o_ref[r, :] = _dot(h_ref[r, :], w_ref[...]).astype(o_ref.dtype)
            if has_small:
                os_ref[r, :] = _dot(h_ref[r, :], ws_ref[...])

    @pl.when(j > 0)
    def _():
        o_ref[...] = _dot(h_ref[...], w_ref[...]).astype(o_ref.dtype)


def _mm_in(x, mod, gain, w, w_small=None, tm=1024, tn=1152):
    b, l, d = x.shape
    n = w.shape[1]
    m = b * l
    nb = mod.shape[0]
    tm = min(tm, l if nb > 1 else m)
    tiles_per_batch = l // tm
    mod_idx = (lambda i, j: (i // tiles_per_batch, 0, 0)) if nb > 1 else (lambda i, j: (0, 0, 0))
    has_small = w_small is not None
    in_specs = [
        pl.BlockSpec((tm, d), lambda i, j: (i, 0)),
        pl.BlockSpec((None, MOD_COUNT, d), mod_idx),
        pl.BlockSpec((1, d), lambda i, j: (0, 0)),
        pl.BlockSpec((d, tn), lambda i, j: (0, j)),
    ]
    args = [x.reshape(m, d), mod, gain, w]
    out_specs = [pl.BlockSpec((tm, tn), lambda i, j: (i, j))]
    out_shape = [jax.ShapeDtypeStruct((m, n), BF16)]
    if has_small:
        ns = w_small.shape[1]
        in_specs.append(pl.BlockSpec((d, ns), lambda i, j: (0, 0)))
        args.append(w_small)
        out_specs.append(pl.BlockSpec((tm, ns), lambda i, j: (i, 0)))
        out_shape.append(jax.ShapeDtypeStruct((m, ns), F32))
    res = pl.pallas_call(
        functools.partial(_mm_in_kernel, has_small=has_small),
        grid=(m // tm, n // tn),
        in_specs=in_specs,
        out_specs=out_specs,
        out_shape=out_shape,
        scratch_shapes=[pltpu.VMEM((tm, d), BF16)],
        compiler_params=_cparams(("arbitrary", "arbitrary")),
        name="mm_in",
    )(*args)
    if has_small:
        return res[0].reshape(b, l, n), res[1].reshape(b, l, -1)
    return res[0].reshape(b, l, n)


def _mm_out_kernel(*refs, n_lhs):
    lhs_refs = refs[:n_lhs]
    w_ref, x_ref, mod_ref, g_ref, o_ref = refs[n_lhs:]
    gate = mod_ref[2:3, :]
    for r in _row_halves(x_ref.shape[0]):
        acc = None
        k0 = 0
        for a_ref in lhs_refs:
            kk = a_ref.shape[1]
            part = _dot(a_ref[r, :].astype(BF16), w_ref[k0:k0 + kk, :])
            acc = part if acc is None else acc + part
            k0 += kk
        o_ref[r, :] = x_ref[r, :] + gate * _rms(acc, g_ref[...])


def _mm_out(lhs_list, w, x, mod, gain, tm=512):
    b, l, d = x.shape
    m = b * l
    nb = mod.shape[0]
    tm = min(tm, l if nb > 1 else m)
    tiles_per_batch = l // tm
    mod_idx = (lambda i: (i // tiles_per_batch, 0, 0)) if nb > 1 else (lambda i: (0, 0, 0))
    ktot = w.shape[0]
    in_specs = [pl.BlockSpec((tm, a.shape[-1]), lambda i: (i, 0)) for a in lhs_list]
    in_specs += [
        pl.BlockSpec((ktot, d), lambda i: (0, 0), pipeline_mode=pl.Buffered(1)),
        pl.BlockSpec((tm, d), lambda i: (i, 0)),
        pl.BlockSpec((None, MOD_COUNT, d), mod_idx),
        pl.BlockSpec((1, d), lambda i: (0, 0)),
    ]
    out = pl.pallas_call(
        functools.partial(_mm_out_kernel, n_lhs=len(lhs_list)),
        grid=(m // tm,),
        in_specs=in_specs,
        out_specs=pl.BlockSpec((tm, d), lambda i: (i, 0)),
        out_shape=jax.ShapeDtypeStruct((m, d), F32),
        compiler_params=_cparams(("arbitrary",)),
        name="mm_out",
    )(*[a.reshape(m, a.shape[-1]) for a in lhs_list], w, x.reshape(m, d), mod, gain)
    return out.reshape(b, l, d)


def _ffn_kernel(x_ref, mod_ref, g_ref, wg_ref, wu_ref, wd_ref, o_ref, h_ref, acc_ref):
    f = pl.program_id(1)
    nf = pl.num_programs(1)
    halves = _row_halves(x_ref.shape[0])

    def down(rows):
        h = h_ref[rows, :]
        a = _silu(_dot(h, wg_ref[...])) * _dot(h, wu_ref[...])
        return _dot(a.astype(BF16), wd_ref[...])

    @pl.when(f == 0)
    def _():
        for r in halves:
            h_ref[r, :] = _modnorm(x_ref[r, :], mod_ref, g_ref[0:1, :], 1).astype(BF16)
            acc_ref[r, :] = down(r)

    @pl.when((f > 0) & (f < nf - 1))
    def _():
        acc_ref[...] += down(slice(None))

    @pl.when(f == nf - 1)
    def _():
        gate = mod_ref[5:6, :]
        for r in halves:
            o_ref[r, :] = x_ref[r, :] + gate * _rms(acc_ref[r, :] + down(r), g_ref[1:2, :])


def _ffn(x, mod, gains, wg, wu, wd, tm=512, tf=512):
    b, l, d = x.shape
    m = b * l
    ff = wg.shape[1]
    nb = mod.shape[0]
    tm = min(tm, l if nb > 1 else m)
    tiles_per_batch = l // tm
    mod_idx = (lambda i, f: (i // tiles_per_batch, 0, 0)) if nb > 1 else (lambda i, f: (0, 0, 0))
    out = pl.pallas_call(
        _ffn_kernel,
        grid=(m // tm, ff // tf),
        in_specs=[
            pl.BlockSpec((tm, d), lambda i, f: (i, 0)),
            pl.BlockSpec((None, MOD_COUNT, d), mod_idx),
            pl.BlockSpec((2, d), lambda i, f: (0, 0)),
            pl.BlockSpec((d, tf), lambda i, f: (0, f)),
            pl.BlockSpec((d, tf), lambda i, f: (0, f)),
            pl.BlockSpec((tf, d), lambda i, f: (f, 0)),
        ],
        out_specs=pl.BlockSpec((tm, d), lambda i, f: (i, 0)),
        out_shape=jax.ShapeDtypeStruct((m, d), F32),
        scratch_shapes=[pltpu.VMEM((tm, d), BF16), pltpu.VMEM((tm, d), F32)],
        compiler_params=_cparams(("arbitrary", "arbitrary")),
        name="ffn",
    )(x.reshape(m, d), mod, gains, wg, wu, wd)
    return out.reshape(b, l, d)


def _rope_rotate(x, cos, sin):
    half = x.shape[1] // 2
    swapped = jnp.concatenate(
        [pltpu.roll(x[:, :half], half // 2, axis=1), pltpu.roll(x[:, half:], half // 2, axis=1)], axis=1)
    return x * cos + swapped * sin


def _gla_kernel(*refs, seq, rope, has_s0, want_state):
    it = iter(refs)
    q_ref, k_ref, v_ref, g_ref, lr_ref, wup_ref, bup_ref, gn_ref = [next(it) for _ in range(8)]
    cos_ref = sin_ref = s0_ref = snew_ref = None
    if rope:
        cos_ref, sin_ref = next(it), next(it)
    if has_s0:
        s0_ref = next(it)
    o_ref = next(it)
    if want_state:
        snew_ref = next(it)
    s_ref, oacc_ref, qd_ref, k2_ref, ec_ref = [next(it) for _ in range(5)]

    cs = GLA_CHUNK
    nc = seq // cs
    _, dk, dv = s_ref.shape
    grp = GLA_PREPASS_CHUNKS
    rg = grp * cs
    ri = lax.broadcasted_iota(jnp.int32, (rg, rg), 0)
    ci = lax.broadcasted_iota(jnp.int32, (rg, rg), 1)
    same_chunk = (ri // cs) == (ci // cs)
    masks = (same_chunk & (ri >= ci), same_chunk & (ri <= ci))
    masks_bf = tuple(jnp.where(m, 1.0, 0.0).astype(BF16) for m in masks)

    def prepass(gi, carry):
        rows = pl.ds(pl.multiple_of(gi * rg, rg), rg)
        q = q_ref[rows, :].astype(F32) * (dk ** -0.5)
        k = k_ref[rows, :].astype(F32)
        if rope:
            cos, sin = cos_ref[rows, :], sin_ref[rows, :]
            q = _rope_rotate(q, cos, sin)
            k = _rope_rotate(k, cos, sin)
        v = v_ref[rows, :]
        lr = lr_ref[rows, :]
        intra = None
        for dirn in range(2):
            la = _log_sigmoid(_dot(lr, wup_ref[dirn]) + bup_ref[dirn]) * (LOG2_E / GLA_TAU)
            b = _dot_exact_rhs(masks_bf[dirn], la)
            last = cs - 1 if dirn == 0 else 0
            totals = [b[j * cs + last:j * cs + last + 1, :] for j in range(grp)]
            b_all = jnp.concatenate([jnp.broadcast_to(t, (cs, dk)) for t in totals], axis=0)
            qd = (q * jnp.exp2(b)).astype(BF16)
            kd = (k * jnp.exp2(-b)).astype(BF16)
            qd_ref[dirn, rows, :] = qd
            k2_ref[dirn, rows, :] = (k * jnp.exp2(b_all - b)).astype(BF16)
            for j in range(grp):
                ec_ref[dirn, gi * grp + j] = _row_to_col(jnp.exp2(totals[j]))
            sc = jnp.where(masks[dirn], _dot_nt(qd, kd), 0.0)
            part = _dot(sc.astype(BF16), v)
            intra = part if intra is None else intra + part
        oacc_ref[rows, :] = intra
        return carry

    lax.fori_loop(0, seq // rg, prepass, 0, unroll=math.gcd(seq // rg, 4))

    def step(c, dirn):
        rows = pl.ds(pl.multiple_of(c * cs, cs), cs)
        s_old = s_ref[dirn]
        inter = _dot(qd_ref[dirn, rows, :], s_old.astype(BF16))
        decay = jnp.concatenate([ec_ref[dirn, c]] * (dv // LANES), axis=1)
        s_ref[dirn] = decay * s_old + _dot_tn(k2_ref[dirn, rows, :], v_ref[rows, :])
        return rows, inter

    for dirn in range(2):
        if has_s0:
            s_ref[dirn] = s0_ref[dirn]
        else:
            s_ref[dirn] = jnp.zeros((dk, dv), F32)

    def first_half(i, carry):
        for dirn, c in ((0, i), (1, nc - 1 - i)):
            rows, inter = step(c, dirn)
            oacc_ref[rows, :] += inter
        return carry

    def second_half(i, carry):
        for dirn, c in ((0, i), (1, nc - 1 - i)):
            rows, inter = step(c, dirn)
            o = _rms(oacc_ref[rows, :] + inter, gn_ref[...]) * _silu(g_ref[rows, :].astype(F32))
            o_ref[rows, :] = o.astype(o_ref.dtype)
        return carry

    half_unroll = 2 if nc % 4 == 0 else 1
    lax.fori_loop(0, nc // 2, first_half, 0, unroll=half_unroll)
    lax.fori_loop(nc // 2, nc, second_half, 0, unroll=half_unroll)
    if want_state:
        for dirn in range(2):
            snew_ref[dirn] = s_ref[dirn]


def _gla(proj, col, wup, bup, gn, rope_tabs, s0, want_state):
    b, l, _ = proj.shape
    h = GLA_HEADS
    dk = wup.shape[2] // h
    dv = gn.shape[1] // h
    nc = l // GLA_CHUNK
    rope = rope_tabs is not None
    has_s0 = s0 is not None
    in_specs = [
        pl.BlockSpec((None, l, dk), lambda bi, hi: (bi, 0, col["q"] // dk + hi)),
        pl.BlockSpec((None, l, dk), lambda bi, hi: (bi, 0, col["k"] // dk + hi)),
        pl.BlockSpec((None, l, dv), lambda bi, hi: (bi, 0, col["v"] // dv + hi)),
        pl.BlockSpec((None, l, dv), lambda bi, hi: (bi, 0, col["g"] // dv + hi)),
        pl.BlockSpec((None, l, LANES), lambda bi, hi: (bi, 0, col["lr"] // LANES)),
        pl.BlockSpec((2, LANES, dk), lambda bi, hi: (0, 0, hi)),
        pl.BlockSpec((2, 1, dk), lambda bi, hi: (0, 0, hi)),
        pl.BlockSpec((1, dv), lambda bi, hi: (0, hi)),
    ]
    args = [proj, proj, proj, proj, proj, wup, bup, gn]
    if rope:
        in_specs += [pl.BlockSpec((l, dk), lambda bi, hi: (0, 0), pipeline_mode=pl.Buffered(1))] * 2
        args += list(rope_tabs)
    if has_s0:
        in_specs.append(pl.BlockSpec((None, 2, None, dk, dv), lambda bi, hi: (bi, 0, hi, 0, 0)))
        args.append(s0)
    out_specs = [pl.BlockSpec((None, l, dv), lambda bi, hi: (bi, 0, hi))]
    out_shape = [jax.ShapeDtypeStruct((b, l, h * dv), BF16)]
    if want_state:
        out_specs.append(pl.BlockSpec((None, 2, None, dk, dv), lambda bi, hi: (bi, 0, hi, 0, 0)))
        out_shape.append(jax.ShapeDtypeStruct((b, 2, h, dk, dv), F32))
    res = pl.pallas_call(
        functools.partial(_gla_kernel, seq=l, rope=rope, has_s0=has_s0, want_state=want_state),
        grid=(b, h),
        in_specs=in_specs,
        out_specs=out_specs,
        out_shape=out_shape,
        scratch_shapes=[
            pltpu.VMEM((2, dk, dv), F32), pltpu.VMEM((l, dv), F32),
            pltpu.VMEM((2, l, dk), BF16), pltpu.VMEM((2, l, dk), BF16),
            pltpu.VMEM((2, nc, dk, LANES), F32),
        ],
        compiler_params=_cparams(("arbitrary", "arbitrary")),
        name="gla",
    )(*args)
    return res if want_state else res[0]


def _tile_scan(a, u, reverse):
    n = a.shape[0]
    rid = lax.broadcasted_iota(jnp.int32, a.shape, 0)
    for d in (1, 2, 4):
        shift, m = (n - d, rid < n - d) if reverse else (d, rid >= d)
        a_s = pltpu.roll(a, shift, axis=0)
        u_s = pltpu.roll(u, shift, axis=0)
        u = jnp.where(m, u + a * u_s, u)
        a = jnp.where(m, a * a_s, a)
    return a, u


def _conv4(pad_ref, r0, n, halo, w_ref, b_ref):
    xw = pad_ref[pl.ds(r0, n + 2 * halo), :]
    tot = n + 2 * halo
    return (w_ref[0:1, :] * pltpu.roll(xw, 2, axis=0)[halo:halo + n]
            + w_ref[1:2, :] * pltpu.roll(xw, 1, axis=0)[halo:halo + n]
            + w_ref[2:3, :] * xw[halo:halo + n]
            + w_ref[3:4, :] * pltpu.roll(xw, tot - 1, axis=0)[halo:halo + n]
            + b_ref[...])


def _fill_padded(pad_ref, src_ref, seq, halo, step):
    wd = pad_ref.shape[1]
    pad_ref[0:halo, :] = jnp.zeros((halo, wd), F32)
    pad_ref[seq + halo:seq + 2 * halo, :] = jnp.zeros((halo, wd), F32)

    def body(i, carry):
        r0 = pl.multiple_of(i * step, step)
        pad_ref[pl.ds(r0 + halo, step), :] = src_ref[pl.ds(r0, step), :].astype(F32)
        return carry

    lax.fori_loop(0, seq // step, body, 0)


def _rglru_kernel(*refs, seq, has_h0, want_state, tc):
    it = iter(refs)
    xr_ref, yr_ref, cw_ref, cb_ref, wg_ref, bg_ref, lam_ref = [next(it) for _ in range(7)]
    h0_ref = next(it) if has_h0 else None
    y_ref = next(it)
    hl_ref = next(it) if want_state else None
    xpad_ref, af_ref, uf_ref, ab_ref, ub_ref, gy_ref = [next(it) for _ in range(6)]

    wb = xr_ref.shape[1]
    nblk = wb // RNN_BLOCK
    halo = 8
    _fill_padded(xpad_ref, xr_ref, seq, halo, tc)
    ls = _log_sigmoid(lam_ref[...])

    def gates(c, carry):
        r0 = pl.multiple_of(c * tc, tc)
        rows = pl.ds(r0, tc)
        xc = _conv4(xpad_ref, r0, tc, halo, cw_ref, cb_ref)
        gy_ref[rows, :] = _gelu_tanh(yr_ref[rows, :].astype(F32))
        for j in range(nblk):
            lanes = slice(j * RNN_BLOCK, (j + 1) * RNN_BLOCK)
            xcj = xc[:, lanes]
            gt = _dot(xcj.astype(BF16), wg_ref[j]) + bg_ref[j]
            for dirn, (a_ref, u_ref) in enumerate(((af_ref, uf_ref), (ab_ref, ub_ref))):
                r = _sigmoid(gt[:, dirn * RNN_BLOCK:(dirn + 1) * RNN_BLOCK])
                i = _sigmoid(gt[:, (2 + dirn) * RNN_BLOCK:(3 + dirn) * RNN_BLOCK])
                log_a = RNN_C * r * ls[dirn:dirn + 1, lanes]
                a = jnp.exp(log_a)
                u = jnp.sqrt(-jnp.tanh(log_a) * (a * a + 1.0)) * i * xcj
                a_ref[rows, lanes] = a
                u_ref[rows, lanes] = u
        return carry

    lax.fori_loop(0, seq // tc, gates, 0)

    h0f = h0_ref[0:1, :] if has_h0 else jnp.zeros((1, wb), F32)
    h0b = h0_ref[1:2, :] if has_h0 else jnp.zeros((1, wb), F32)

    def fwd(t, hc):
        rows = pl.ds(pl.multiple_of(t * 8, 8), 8)
        a, u = _tile_scan(af_ref[rows, :], uf_ref[rows, :], False)
        hh = u + a * hc
        uf_ref[rows, :] = hh
        return hh[7:8, :]

    hlf = lax.fori_loop(0, seq // 8, fwd, h0f, unroll=4)

    n16 = seq // 16

    def bwd(i, hc):
        r16 = pl.multiple_of((n16 - 1 - i) * 16, 16)
        lo, hi = pl.ds(r16, 8), pl.ds(r16 + 8, 8)
        a1, u1 = _tile_scan(ab_ref[hi, :], ub_ref[hi, :], True)
        h1 = u1 + a1 * hc
        a0, u0 = _tile_scan(ab_ref[lo, :], ub_ref[lo, :], True)
        h0 = u0 + a0 * h1[0:1, :]
        y = jnp.concatenate([(uf_ref[lo, :] + h0) * gy_ref[lo, :], (uf_ref[hi, :] + h1) * gy_ref[hi, :]], axis=0)
        y_ref[pl.ds(r16, 16), :] = y.astype(y_ref.dtype)
        return h0[0:1, :]

    hlb = lax.fori_loop(0, n16, bwd, h0b, unroll=2)
    if want_state:
        hl_ref[0:1, :] = hlf
        hl_ref[1:2, :] = hlb


def _rglru(proj, col, cw, cb, wg, bg, lam, h0, want_state, wb=512, tc=256):
    b, l, _ = proj.shape
    w = cw.shape[1]
    tc = min(tc, l)
    nblk = wb // RNN_BLOCK
    has_h0 = h0 is not None
    in_specs = [
        pl.BlockSpec((None, l, wb), lambda bi, j: (bi, 0, col["xr"] // wb + j)),
        pl.BlockSpec((None, l, wb), lambda bi, j: (bi, 0, col["yr"] // wb + j)),
        pl.BlockSpec((4, wb), lambda bi, j: (0, j)),
        pl.BlockSpec((1, wb), lambda bi, j: (0, j)),
        pl.BlockSpec((nblk, RNN_BLOCK, 4 * RNN_BLOCK), lambda bi, j: (j, 0, 0)),
        pl.BlockSpec((nblk, 1, 4 * RNN_BLOCK), lambda bi, j: (j, 0, 0)),
        pl.BlockSpec((2, wb), lambda bi, j: (0, j)),
    ]
    args = [proj, proj, cw, cb, wg, bg, lam]
    if has_h0:
        in_specs.append(pl.BlockSpec((None, 2, wb), lambda bi, j: (bi, 0, j)))
        args.append(h0)
    out_specs = [pl.BlockSpec((None, l, wb), lambda bi, j: (bi, 0, j))]
    out_shape = [jax.ShapeDtypeStruct((b, l, w), BF16)]
    if want_state:
        out_specs.append(pl.BlockSpec((None, 2, wb), lambda bi, j: (bi, 0, j)))
        out_shape.append(jax.ShapeDtypeStruct((b, 2, w), F32))
    res = pl.pallas_call(
        functools.partial(_rglru_kernel, seq=l, has_h0=has_h0, want_state=want_state, tc=tc),
        grid=(b, w // wb),
        in_specs=in_specs,
        out_specs=out_specs,
        out_shape=out_shape,
        scratch_shapes=[pltpu.VMEM((l + 16, wb), F32)] + [pltpu.VMEM((l, wb), F32)] * 5,
        compiler_params=_cparams(("arbitrary", "arbitrary")),
        name="rglru",
    )(*args)
    return res if want_state else res[0]


def _ssd_dt_kernel(dt_ref, bias_ref, alog_ref, cf_ref, cb_ref, qf_ref, qb_ref, cfr_ref, cbr_ref, qfr_ref, qbr_ref,
                   *, seq):
    cs = SSD_CHUNK
    tril, triu = _tri_masks(cs)
    tril_bf = jnp.where(tril, 1.0, 0.0).astype(BF16)
    triu_bf = jnp.where(triu, 1.0, 0.0).astype(BF16)
    a = -jnp.exp(alog_ref[...])

    def body(c, carry):
        rows = pl.ds(pl.multiple_of(c * cs, cs), cs)
        dt = _softplus(dt_ref[rows, :] + bias_ref[...])
        la = dt * a
        log_dt = jnp.log(dt)
        cf = _dot_exact_rhs(tril_bf, la)
        cb = _dot_exact_rhs(triu_bf, la)
        for val, col_ref, row_ref in ((cf, cf_ref, cfr_ref), (cb, cb_ref, cbr_ref),
                                      (cf - log_dt, qf_ref, qfr_ref), (cb - log_dt, qb_ref, qbr_ref)):
            val = val * LOG2_E
            col_ref[rows, :] = val
            row_ref[c] = val.T
        return carry

    lax.fori_loop(0, seq // cs, body, 0)


def _ssd_dt(dt_raw, bias, alog):
    b, l, _ = dt_raw.shape
    nc = l // SSD_CHUNK
    cshp = jax.ShapeDtypeStruct((b, l, LANES), F32)
    cspec = pl.BlockSpec((None, l, LANES), lambda bi: (bi, 0, 0))
    rshp = jax.ShapeDtypeStruct((b, nc, LANES, SSD_CHUNK), F32)
    rspec = pl.BlockSpec((None, nc, LANES, SSD_CHUNK), lambda bi: (bi, 0, 0, 0))
    return pl.pallas_call(
        functools.partial(_ssd_dt_kernel, seq=l),
        grid=(b,),
        in_specs=[
            pl.BlockSpec((None, l, LANES), lambda bi: (bi, 0, 0)),
            pl.BlockSpec((1, LANES), lambda bi: (0, 0)),
            pl.BlockSpec((1, LANES), lambda bi: (0, 0)),
        ],
        out_specs=[cspec] * 4 + [rspec] * 4,
        out_shape=[cshp] * 4 + [rshp] * 4,
        compiler_params=_cparams(("arbitrary",)),
        name="ssd_dt",
    )(dt_raw, bias, alog)


def _ssd_kernel(*refs, seq, has_s0, want_state):
    it = iter(refs)
    (z_ref, x_ref, bm_ref, cm_ref, cfc_ref, cbc_ref, qfc_ref, qbc_ref, cfr_ref, cbr_ref, qfr_ref, qbr_ref,
     cwx_ref, cwb_ref, cwc_ref, cbx_ref, cbb_ref, cbc2_ref, dsk_ref, gn_ref) = [next(it) for _ in range(20)]
    s0_ref = next(it) if has_s0 else None
    y_ref = next(it)
    snew_ref = next(it) if want_state else None
    (xpad_ref, bpad_ref, cpad_ref, xs_ref, bt_ref, cs_ref, cbm_ref, yacc_ref, st_ref) = [next(it) for _ in range(9)]
    cum_cols, cum_rows = (cfc_ref, cbc_ref), (cfr_ref, cbr_ref)
    q_cols, q_rows = (qfc_ref, qbc_ref), (qfr_ref, qbr_ref)

    cs = SSD_CHUNK
    nc = seq // cs
    hp = SSD_HEAD_DIM
    gw = x_ref.shape[1]
    hg = gw // hp
    npair = gw // LANES
    halo = 8
    _fill_padded(xpad_ref, x_ref, seq, halo, cs)
    _fill_padded(bpad_ref, bm_ref, seq, halo, cs)
    _fill_padded(cpad_ref, cm_ref, seq, halo, cs)

    def conv(c, carry):
        r0 = pl.multiple_of(c * cs, cs)
        rows = pl.ds(r0, cs)
        xs_ref[rows, :] = _silu(_conv4(xpad_ref, r0, cs, halo, cwx_ref, cbx_ref))
        bc = _silu(_conv4(bpad_ref, r0, cs, halo, cwb_ref, cbb_ref))
        cc = _silu(_conv4(cpad_ref, r0, cs, halo, cwc_ref, cbc2_ref)).astype(BF16)
        bt_ref[c] = bc.T.astype(BF16)
        cs_ref[rows, :] = cc
        cbm_ref[rows, :] = _dot_nt(cc, bc.astype(BF16))
        return carry

    lax.fori_loop(0, nc, conv, 0, unroll=2)

    masks = _tri_masks(cs)
    lane = lax.broadcasted_iota(jnp.int32, (cs, LANES), 1)
    first_of_pair = lane < hp
    gbase = pl.program_id(1) * (2 * hg)

    def chunk(c, dirn):
        rows = pl.ds(pl.multiple_of(c * cs, cs), cs)
        xs = xs_ref[rows, :]
        c_c = cs_ref[rows, :]
        cbm = cbm_ref[rows, :]
        cum_t = cum_cols[dirn][rows, :]
        q_t = q_cols[dirn][rows, :]
        q_r = q_rows[dirn][c]
        last = cs - 1 if dirn == 0 else 0
        end_lane = cum_t[last:last + 1, :]
        w_all = jnp.exp2(end_lane - q_t)
        e_end = jnp.exp2(jnp.broadcast_to(end_lane, (8, LANES)))
        s_old = st_ref[dirn]
        y_state = _dot(c_c, s_old.astype(BF16))
        tiles, xw_tiles, end_tiles = [], [], []
        for m in range(npair):
            h0 = dirn * hg + 2 * m
            idx = gbase + h0 + lane // hp
            xp = xs[:, m * LANES:(m + 1) * LANES]
            xw_tiles.append((xp * jnp.take_along_axis(w_all, idx, axis=1)).astype(BF16))
            end_tiles.append(jnp.take_along_axis(e_end, idx[0:8], axis=1)[0:1])
            rhs = jnp.concatenate([jnp.where(first_of_pair, xp, 0.0), jnp.where(first_of_pair, 0.0, xp)],
                                  axis=0).astype(BF16)
            lhs, ecum = [], []
            for jj in range(2):
                col = jnp.take_along_axis(cum_t, jnp.zeros_like(lane) + (gbase + h0 + jj), axis=1)
                seg = col - q_r[h0 + jj:h0 + jj + 1, :]
                lhs.append((cbm * jnp.exp2(jnp.where(masks[dirn], seg, -jnp.inf))).astype(BF16))
                ecum.append(jnp.exp2(col))
            y_p = _dot(jnp.concatenate(lhs, axis=1), rhs)
            y_p = y_p + y_state[:, m * LANES:(m + 1) * LANES] * jnp.where(first_of_pair, ecum[0], ecum[1])
            tiles.append(y_p)
        xw = jnp.concatenate(xw_tiles, axis=1)
        st_ref[dirn] = s_old * jnp.concatenate(end_tiles, axis=1) + _dot(bt_ref[c], xw)
        return rows, jnp.concatenate(tiles, axis=1)

    for dirn in range(2):
        if has_s0:
            st_ref[dirn] = s0_ref[dirn].T
        else:
            st_ref[dirn] = jnp.zeros(st_ref.shape[1:], F32)

    def finish(rows, y):
        y = y + dsk_ref[...] * xs_ref[rows, :]
        y = y * _silu(z_ref[rows, :].astype(F32))
        y_ref[rows, :] = _rms(y, gn_ref[...]).astype(y_ref.dtype)

    def first_half(i, carry):
        for dirn, c in ((0, i), (1, nc - 1 - i)):
            rows, y = chunk(c, dirn)
            yacc_ref[rows, :] = y
        return carry

    def second_half(i, carry):
        for dirn, c in ((0, i), (1, nc - 1 - i)):
            rows, y = chunk(c, dirn)
            finish(rows, yacc_ref[rows, :] + y)
        return carry

    half_unroll = 2 if nc % 4 == 0 else 1
    lax.fori_loop(0, nc // 2, first_half, 0, unroll=half_unroll)
    lax.fori_loop(nc // 2, nc, second_half, 0, unroll=half_unroll)
    if want_state:
        for dirn in range(2):
            snew_ref[dirn] = st_ref[dirn].T


def _ssd(proj, col, cums, cw, cb, dsk, gn, s0, want_state):
    b, l, _ = proj.shape
    g = SSD_GROUPS
    inner = dsk.shape[1]
    gw = inner // g
    ns = SSD_STATE
    nc = l // SSD_CHUNK
    hg2 = 2 * gw // SSD_HEAD_DIM
    has_s0 = s0 is not None
    bcol0 = inner // ns
    ccol0 = bcol0 + g
    assert nc % 2 == 0
    colspec = pl.BlockSpec((None, l, LANES), lambda bi, gi: (bi, 0, 0))
    rowspec = pl.BlockSpec((None, nc, hg2, SSD_CHUNK), lambda bi, gi: (bi, 0, gi, 0))
    in_specs = [
        pl.BlockSpec((None, l, gw), lambda bi, gi: (bi, 0, col["z"] // gw + gi)),
        pl.BlockSpec((None, l, gw), lambda bi, gi: (bi, 0, col["x"] // gw + gi)),
        pl.BlockSpec((None, l, ns), lambda bi, gi: (bi, 0, col["B"] // ns + gi)),
        pl.BlockSpec((None, l, ns), lambda bi, gi: (bi, 0, col["C"] // ns + gi)),
        colspec, colspec, colspec, colspec, rowspec, rowspec, rowspec, rowspec,
        pl.BlockSpec((4, gw), lambda bi, gi: (0, gi)),
        pl.BlockSpec((4, ns), lambda bi, gi: (0, bcol0 + gi)),
        pl.BlockSpec((4, ns), lambda bi, gi: (0, ccol0 + gi)),
        pl.BlockSpec((1, gw), lambda bi, gi: (0, gi)),
        pl.BlockSpec((1, ns), lambda bi, gi: (0, bcol0 + gi)),
        pl.BlockSpec((1, ns), lambda bi, gi: (0, ccol0 + gi)),
        pl.BlockSpec((1, gw), lambda bi, gi: (0, gi)),
        pl.BlockSpec((1, gw), lambda bi, gi: (0, gi)),
    ]
    args = [proj, proj, proj, proj, *cums, cw, cw, cw, cb, cb, cb, dsk, gn]
    if has_s0:
        in_specs.append(pl.BlockSpec((None, 2, gw, ns), lambda bi, gi: (bi, 0, gi, 0)))
        args.append(s0)
    out_specs = [pl.BlockSpec((None, l, gw), lambda bi, gi: (bi, 0, gi))]
    out_shape = [jax.ShapeDtypeStruct((b, l, inner), BF16)]
    if want_state:
        out_specs.append(pl.BlockSpec((None, 2, gw, ns), lambda bi, gi: (bi, 0, gi, 0)))
        out_shape.append(jax.ShapeDtypeStruct((b, 2, inner, ns), F32))
    res = pl.pallas_call(
        functools.partial(_ssd_kernel, seq=l, has_s0=has_s0, want_state=want_state),
        grid=(b, g),
        in_specs=in_specs,
        out_specs=out_specs,
        out_shape=out_shape,
        scratch_shapes=[
            pltpu.VMEM((l + 16, gw), F32), pltpu.VMEM((l + 16, ns), F32), pltpu.VMEM((l + 16, ns), F32),
            pltpu.VMEM((l, gw), F32), pltpu.VMEM((nc, ns, SSD_CHUNK), BF16), pltpu.VMEM((l, ns), BF16),
            pltpu.VMEM((l, SSD_CHUNK), F32), pltpu.VMEM((l, gw), F32), pltpu.VMEM((2, ns, gw), F32),
        ],
        compiler_params=_cparams(("arbitrary", "arbitrary")),
        name="ssd",
    )(*args)
    return res if want_state else res[0]


def _prep_even(w_in_all, layer, w_up, b_up, w_r, b_r, w_i, b_i):
    qk = w_up.shape[2]
    gv = (w_in_all.shape[2] - 2 * qk - 2 * GLA_RANK) // 4
    sizes = (qk, qk, gv, gv, 2 * GLA_RANK, gv, gv)
    offs = [0]
    for s in sizes:
        offs.append(offs[-1] + s)
    w = _even_w_in(w_in_all, layer, offs[4], offs[5])
    col = {"q": 0, "k": qk, "v": 2 * qk, "g": 2 * qk + gv, "xr": 2 * qk + 2 * gv, "yr": 2 * qk + 3 * gv,
           "lr": 2 * qk + 4 * gv}
    wup = jnp.zeros((2, LANES, qk), F32)
    wup = wup.at[0, 0:GLA_RANK].set(w_up[0]).at[1, GLA_RANK:2 * GLA_RANK].set(w_up[1])
    nblk = w_r.shape[1]
    wg = jnp.concatenate([w_r[0], w_r[1], w_i[0], w_i[1]], axis=-1).astype(BF16)
    bg = jnp.concatenate([b.reshape(nblk, 1, RNN_BLOCK) for b in (b_r[0], b_r[1], b_i[0], b_i[1])], axis=-1)
    return {
        "w_in": w, "col": col,
        "wup": wup.astype(BF16), "bup": b_up.reshape(2, 1, qk), "wg": wg, "bg": bg,
    }


def _group_major(a, heads):
    g = SSD_GROUPS
    hg = heads // g
    lead = a.shape[:-1]
    return jnp.swapaxes(a.reshape(lead + (2, g, hg)), -3, -2).reshape(lead + (2 * heads,))


def _prep_odd(w_in_all, layer, dt_bias, a_log, d_skip):
    heads = d_skip.shape[0]
    inner = heads * SSD_HEAD_DIM
    xbc = inner + 2 * SSD_GROUPS * SSD_STATE
    dtw = _group_major(w_in_all[layer][:, inner + xbc:], heads).astype(BF16)
    col = {"z": 0, "x": inner, "B": 2 * inner, "C": 2 * inner + SSD_GROUPS * SSD_STATE}
    return {
        "w_in": _cast_bf16(w_in_all, layer, cols=inner + xbc), "w_dt": dtw, "col": col,
        "dt_bias": _group_major(dt_bias.reshape(1, 2 * heads), heads),
        "a_log": _group_major(a_log.reshape(1, 2 * heads), heads),
        "dsk": jnp.repeat(d_skip, SSD_HEAD_DIM).reshape(1, inner),
    }


def _rope_tables(seq, dk):
    nf = dk // 4
    inv = ROPE_BASE ** (-jnp.arange(nf, dtype=F32) / nf)
    t = jnp.arange(seq)
    ang_r = (t // GRID_W).astype(F32)[:, None] * inv
    ang_c = (t % GRID_W).astype(F32)[:, None] * inv
    cos = jnp.concatenate([jnp.cos(ang_r)] * 2 + [jnp.cos(ang_c)] * 2, axis=1)
    sin = jnp.concatenate([-jnp.sin(ang_r), jnp.sin(ang_r), -jnp.sin(ang_c), jnp.sin(ang_c)], axis=1)
    return cos, sin


def _even_mixer(x, mod, gains, p, gla_gn, cw, cb, lam, s0_gla, h0_rnn, rope_tabs, want_state):
    proj = _mm_in(x, mod, gains[0:1], p["w_in"])
    gla = _gla(proj, p["col"], p["wup"], p["bup"], gla_gn, rope_tabs, s0_gla, want_state)
    rnn = _rglru(proj, p["col"], cw, cb, p["wg"], p["bg"], lam, h0_rnn, want_state)
    if want_state:
        (o, sg), (yr, sr) = gla, rnn
    else:
        (o, sg), (yr, sr) = (gla, None), (rnn, None)
    y = _mm_out([o, yr], p["w_out"], x, mod, gains[1:2])
    return y, sg, sr


def _odd_mixer(x, mod, gains, p, cw, cb, gn, s0, want_state):
    proj, dt_raw = _mm_in(x, mod, gains[0:1], p["w_in"], p["w_dt"], tn=1024)
    cums = _ssd_dt(dt_raw, p["dt_bias"], p["a_log"])
    res = _ssd(proj, p["col"], cums, cw, cb, p["dsk"], gn, s0, want_state)
    yv, ss = res if want_state else (res, None)
    y = _mm_out([yv], p["w_out"], x, mod, gains[1:2])
    return y, ss


def _stack1(xs):
    return xs[0][:, None] if len(xs) == 1 else jnp.stack(xs, axis=1)


def kernel(x_prompt, x_sample, state_gla, state_rglru, state_ssd, c, c_ctx, w_ada, b_ada, norm_g, ev_w_in, ev_w_out,
           gla_w_up, gla_b_up, gla_norm_g, rnn_conv_w, rnn_conv_b, rnn_w_r, rnn_b_r, rnn_w_i, rnn_b_i, rnn_lam,
           od_w_in, od_w_out, ssd_conv_w, ssd_conv_b, ssd_dt_bias, ssd_a_log, ssd_d, ssd_norm_g,
           ffn_w_gate, ffn_w_up, ffn_w_down):
    depth, d = norm_g.shape[0], norm_g.shape[2]
    nb = c.shape[0]
    rows = -(-(nb + 1) // 8) * 8
    cvec = jnp.zeros((rows, d), F32).at[0].set(c_ctx).at[1:nb + 1].set(c)
    mods = _adaln(cvec, w_ada, b_ada).reshape(depth, rows, MOD_COUNT, d)
    rope_tabs = _rope_tables(x_sample.shape[1], gla_w_up.shape[3] // GLA_HEADS)

    yp, ys = x_prompt, x_sample
    new_gla, new_rnn, new_ssd = [], [], []
    for l in range(depth):
        mod_p, mod_s = mods[l, 0:1], mods[l, 1:nb + 1]
        gains = norm_g[l]
        if l % 2 == 0:
            e = l // 2
            p = _prep_even(ev_w_in, e, gla_w_up[e], gla_b_up[e], rnn_w_r[e], rnn_b_r[e], rnn_w_i[e], rnn_b_i[e])
            p["w_out"] = _cast_bf16(ev_w_out, e)
            gn = gla_norm_g[e].reshape(1, -1)
            cw, cb, lam = rnn_conv_w[e], rnn_conv_b[e].reshape(1, -1), rnn_lam[e]
            yp, sg, sr = _even_mixer(yp, mod_p, gains, p, gn, cw, cb, lam, None, None, None, True)
            ys, _, _ = _even_mixer(ys, mod_s, gains, p, gn, cw, cb, lam, state_gla[:, e], state_rglru[:, e],
                                   rope_tabs, False)
            new_gla.append(sg)
            new_rnn.append(sr)
        else:
            o = l // 2
            p = _prep_odd(od_w_in, o, ssd_dt_bias[o], ssd_a_log[o], ssd_d[o])
            p["w_out"] = _cast_bf16(od_w_out, o)
            cw, cb, gn = ssd_conv_w[o], ssd_conv_b[o].reshape(1, -1), ssd_norm_g[o].reshape(1, -1)
            inner, ns = gn.shape[1], SSD_STATE
            s0 = state_ssd[:, o].reshape(nb, 2, inner, ns)
            yp, ss = _odd_mixer(yp, mod_p, gains, p, cw, cb, gn, None, True)
            ys, _ = _odd_mixer(ys, mod_s, gains, p, cw, cb, gn, s0, False)
            new_ssd.append(ss.reshape(ss.shape[0], 2, inner // SSD_HEAD_DIM, SSD_HEAD_DIM, ns))
        wg, wu, wd = _cast_bf16(ffn_w_gate, l), _cast_bf16(ffn_w_up, l), _cast_bf16(ffn_w_down, l)
        yp = _ffn(yp, mod_p, gains[2:4], wg, wu, wd)
        ys = _ffn(ys, mod_s, gains[2:4], wg, wu, wd)
    return (yp, ys, _stack1(new_gla), _stack1(new_rnn), _stack1(new_ssd))
```

```python
import functools
import math

import jax
import jax.numpy as jnp
from jax import lax
from jax.experimental import pallas as pl
from jax.experimental.pallas import tpu as pltpu

F32 = jnp.float32
BF16 = jnp.bfloat16

EPS = 1e-6
MOD_COUNT = 6
GLA_HEADS = 4
GLA_RANK = 16
GLA_TAU = 16.0
GLA_CHUNK = 64
GLA_PREPASS_CHUNKS = 4
ROPE_BASE = 10000.0
GRID_W = 64
RNN_BLOCK = 128
RNN_C = 8.0
SSD_HEAD_DIM = 64
SSD_STATE = 128
SSD_GROUPS = 8
SSD_CHUNK = 128

LOG2_E = math.log2(math.e)
LANES = 128
V7X_VMEM_BYTES = 64 * 1024 * 1024
VMEM_LIMIT = V7X_VMEM_BYTES - 8 * 1024 * 1024


def _cparams(sem):
    return pltpu.CompilerParams(dimension_semantics=sem, vmem_limit_bytes=VMEM_LIMIT)


def _sigmoid(x):
    return 1.0 / (1.0 + jnp.exp(-x))


def _silu(x):
    h = 0.5 * x
    return h + h * jnp.tanh(h)


def _softplus(x):
    return jnp.maximum(x, 0.0) + jnp.log1p(jnp.exp(-jnp.abs(x)))


def _log_sigmoid(x):
    return jnp.minimum(x, 0.0) - jnp.log1p(jnp.exp(-jnp.abs(x)))


def _gelu_tanh(x):
    return 0.5 * x * (1.0 + jnp.tanh(math.sqrt(2.0 / math.pi) * (x + 0.044715 * (x * x * x))))


def _rms(x, gain):
    ms = jnp.mean(x * x, axis=-1, keepdims=True)
    return x * lax.rsqrt(ms + EPS) * gain


def _split3(x):
    hi = x.astype(BF16)
    r1 = x - hi.astype(F32)
    mid = r1.astype(BF16)
    lo = (r1 - mid.astype(F32)).astype(BF16)
    return hi, mid, lo


def _dot(a, b):
    return jnp.dot(a, b, preferred_element_type=F32)


def _dot_nt(a, b):
    return lax.dot_general(a, b, (((1,), (1,)), ((), ())), preferred_element_type=F32)


def _dot_tn(a, b):
    return lax.dot_general(a, b, (((0,), (0,)), ((), ())), preferred_element_type=F32)


def _dot_exact_rhs(a_bf16, x):
    hi, mid, lo = _split3(x)
    return _dot(a_bf16, hi) + _dot(a_bf16, mid) + _dot(a_bf16, lo)


def _tri_masks(n):
    ri = lax.broadcasted_iota(jnp.int32, (n, n), 0)
    ci = lax.broadcasted_iota(jnp.int32, (n, n), 1)
    return ri >= ci, ri <= ci


def _row_halves(n):
    return (slice(0, n // 2), slice(n // 2, n))


def _row_to_col(row):
    cols = []
    for j in range(row.shape[1] // LANES):
        t = jnp.broadcast_to(row[:, j * LANES:(j + 1) * LANES], (16, LANES)).T
        cols.append(jnp.broadcast_to(t[:, 0:1], (LANES, LANES)))
    return jnp.concatenate(cols, axis=0)


def _cast_kernel(x_ref, o_ref):
    o_ref[...] = x_ref[...].astype(o_ref.dtype)


def _cast_bf16(w, layer, cols=None, col_block=0, block_bytes=6 * 1024 * 1024):
    _, r, n = w.shape
    cols = n if cols is None else cols
    tr = r
    while tr * cols * 4 > block_bytes and tr % 16 == 0:
        tr //= 2
    return pl.pallas_call(
        _cast_kernel,
        grid=(r // tr,),
        in_specs=[pl.BlockSpec((None, tr, cols), lambda i: (layer, i, col_block))],
        out_specs=pl.BlockSpec((tr, cols), lambda i: (i, 0)),
        out_shape=jax.ShapeDtypeStruct((r, cols), BF16),
        compiler_params=_cparams(("arbitrary",)),
        name="cast",
    )(w)


def _transpose_cast_kernel(x_ref, o_ref):
    o_ref[...] = x_ref[0].T.astype(o_ref.dtype)


def _even_w_in(w, layer, lo, hi, tc=512):
    _, d, n = w.shape
    n_out = n - (hi - lo)
    assert lo % tc == 0 and n_out % tc == 0 and hi % 8 == 0
    wt = jnp.swapaxes(w, 1, 2)
    src = lambda j: (layer, pl.multiple_of(j * tc + jnp.where(j < lo // tc, 0, hi - lo), 8), 0)
    return pl.pallas_call(
        _transpose_cast_kernel,
        grid=(n_out // tc,),
        in_specs=[pl.BlockSpec((pl.Element(1), pl.Element(tc), pl.Element(d)), src)],
        out_specs=pl.BlockSpec((d, tc), lambda j: (0, j)),
        out_shape=jax.ShapeDtypeStruct((d, n_out), BF16),
        compiler_params=_cparams(("arbitrary",)),
        name="even_w_in",
    )(wt)


def _adaln_kernel(c_ref, w_ref, b_ref, o_ref):
    s = _silu(c_ref[...]).astype(BF16)
    o_ref[...] = _dot(s, w_ref[...].astype(BF16)) + b_ref[...]


def _adaln(cvec, w_ada, b_ada, tn=1024):
    depth, d, n = w_ada.shape
    r = cvec.shape[0]
    return pl.pallas_call(
        _adaln_kernel,
        grid=(depth, n // tn),
        in_specs=[
            pl.BlockSpec((r, d), lambda l, j: (0, 0)),
            pl.BlockSpec((None, d, tn), lambda l, j: (l, 0, j)),
            pl.BlockSpec((None, 1, tn), lambda l, j: (l, 0, j)),
        ],
        out_specs=pl.BlockSpec((None, r, tn), lambda l, j: (l, 0, j)),
        out_shape=jax.ShapeDtypeStruct((depth, r, n), F32),
        compiler_params=_cparams(("arbitrary", "arbitrary")),
        name="adaln",
    )(cvec, w_ada, b_ada.reshape(depth, 1, n))


def _modnorm(x, mod_ref, gain, j):
    shift = mod_ref[3 * j:3 * j + 1, :]
    scale = mod_ref[3 * j + 1:3 * j + 2, :]
    return _rms(x, gain) * (1.0 + scale) + shift


def _mm_in_kernel(*refs, has_small):
    if has_small:
        x_ref, mod_ref, g_ref, w_ref, ws_ref, o_ref, os_ref, h_ref = refs
    else:
        x_ref, mod_ref, g_ref, w_ref, o_ref, h_ref = refs

    j = pl.program_id(1)

    @pl.when(j == 0)
    def _():
        for r in _row_halves(x_ref.shape[0]):
            h_ref[r, :] = _modnorm(x_ref[r, :], mod_ref, g_ref[...], 0).astype(BF16)
            o_ref[r, :] = _dot(h_ref[r, :], w_ref[...]).astype(o_ref.dtype)
            if has_small:
                os_ref[r, :] = _dot(h_ref[r, :], ws_ref[...])

    @pl.when(j > 0)
    def _():
        o_ref[...] = _dot(h_ref[...], w_ref[...]).astype(o_ref.dtype)


def _mm_in(x, mod, gain, w, w_small=None, tm=1024, tn=1152):
    b, l, d = x.shape
    n = w.shape[1]
    m = b * l
    nb = mod.shape[0]
    tm = min(tm, l if nb > 1 else m)
    tiles_per_batch = l // tm
    mod_idx = (lambda i, j: (i // tiles_per_batch, 0, 0)) if nb > 1 else (lambda i, j: (0, 0, 0))
    has_small = w_small is not None
    in_specs = [
        pl.BlockSpec((tm, d), lambda i, j: (i, 0)),
        pl.BlockSpec((None, MOD_COUNT, d), mod_idx),
        pl.BlockSpec((1, d), lambda i, j: (0, 0)),
        pl.BlockSpec((d, tn), lambda i, j: (0, j)),
    ]
    args = [x.reshape(m, d), mod, gain, w]
    out_specs = [pl.BlockSpec((tm, tn), lambda i, j: (i, j))]
    out_shape = [jax.ShapeDtypeStruct((m, n), BF16)]
    if has_small:
        ns = w_small.shape[1]
        in_specs.append(pl.BlockSpec((d, ns), lambda i, j: (0, 0)))
        args.append(w_small)
        out_specs.append(pl.BlockSpec((tm, ns), lambda i, j: (i, 0)))
        out_shape.append(jax.ShapeDtypeStruct((m, ns), F32))
    res = pl.pallas_call(
        functools.partial(_mm_in_kernel, has_small=has_small),
        grid=(m // tm, n // tn),
        in_specs=in_specs,
        out_specs=out_specs,
        out_shape=out_shape,
        scratch_shapes=[pltpu.VMEM((tm, d), BF16)],
        compiler_params=_cparams(("arbitrary", "arbitrary")),
        name="mm_in",
    )(*args)
    if has_small:
        return res[0].reshape(b, l, n), res[1].reshape(b, l, -1)
    return res[0].reshape(b, l, n)


def _mm_out_kernel(*refs, n_lhs):
    lhs_refs = refs[:n_lhs]
    w_ref, x_ref, mod_ref, g_ref, o_ref = refs[n_lhs:]
    gate = mod_ref[2:3, :]
    for r in _row_halves(x_ref.shape[0]):
        acc = None
        k0 = 0
        for a_ref in lhs_refs:
            kk = a_ref.shape[1]
            part = _dot(a_ref[r, :].astype(BF16), w_ref[k0:k0 + kk, :])
            acc = part if acc is None else acc + part
            k0 += kk
        o_ref[r, :] = x_ref[r, :] + gate * _rms(acc, g_ref[...])


def _mm_out(lhs_list, w, x, mod, gain, tm=512):
    b, l, d = x.shape
    m = b * l
    nb = mod.shape[0]
    tm = min(tm, l if nb > 1 else m)
    tiles_per_batch = l // tm
    mod_idx = (lambda i: (i // tiles_per_batch, 0, 0)) if nb > 1 else (lambda i: (0, 0, 0))
    ktot = w.shape[0]
    in_specs = [pl.BlockSpec((tm, a.shape[-1]), lambda i: (i, 0)) for a in lhs_list]
    in_specs += [
        pl.BlockSpec((ktot, d), lambda i: (0, 0), pipeline_mode=pl.Buffered(1)),
        pl.BlockSpec((tm, d), lambda i: (i, 0)),
        pl.BlockSpec((None, MOD_COUNT, d), mod_idx),
        pl.BlockSpec((1, d), lambda i: (0, 0)),
    ]
    out = pl.pallas_call(
        functools.partial(_mm_out_kernel, n_lhs=len(lhs_list)),
        grid=(m // tm,),
        in_specs=in_specs,
        out_specs=pl.BlockSpec((tm, d), lambda i: (i, 0)),
        out_shape=jax.ShapeDtypeStruct((m, d), F32),
        compiler_params=_cparams(("arbitrary",)),
        name="mm_out",
    )(*[a.reshape(m, a.shape[-1]) for a in lhs_list], w, x.reshape(m, d), mod, gain)
    return out.reshape(b, l, d)


def _ffn_kernel(x_ref, mod_ref, g_ref, wg_ref, wu_ref, wd_ref, o_ref, h_ref, acc_ref):
    f = pl.program_id(1)
    nf = pl.num_programs(1)
    halves = _row_halves(x_ref.shape[0])

    def down(rows):
        h = h_ref[rows, :]
        a = _silu(_dot(h, wg_ref[...])) * _dot(h, wu_ref[...])
        return _dot(a.astype(BF16), wd_ref[...])

    @pl.when(f == 0)
    def _():
        for r in halves:
            h_ref[r, :] = _modnorm(x_ref[r, :], mod_ref, g_ref[0:1, :], 1).astype(BF16)
            acc_ref[r, :] = down(r)

    @pl.when((f > 0) & (f < nf - 1))
    def _():
        acc_ref[...] += down(slice(None))

    @pl.when(f == nf - 1)
    def _():
        gate = mod_ref[5:6, :]
        for r in halves:
            o_ref[r, :] = x_ref[r, :] + gate * _rms(acc_ref[r, :] + down(r), g_ref[1:2, :])


def _ffn(x, mod, gains, wg, wu, wd, tm=512, tf=512):
    b, l, d = x.shape
    m = b * l
    ff = wg.shape[1]
    nb = mod.shape[0]
    tm = min(tm, l if nb > 1 else m)
    tiles_per_batch = l // tm
    mod_idx = (lambda i, f: (i // tiles_per_batch, 0, 0)) if nb > 1 else (lambda i, f: (0, 0, 0))
    out = pl.pallas_call(
        _ffn_kernel,
        grid=(m // tm, ff // tf),
        in_specs=[
            pl.BlockSpec((tm, d), lambda i, f: (i, 0)),
            pl.BlockSpec((None, MOD_COUNT, d), mod_idx),
            pl.BlockSpec((2, d), lambda i, f: (0, 0)),
            pl.BlockSpec((d, tf), lambda i, f: (0, f)),
            pl.BlockSpec((d, tf), lambda i, f: (0, f)),
            pl.BlockSpec((tf, d), lambda i, f: (f, 0)),
        ],
        out_specs=pl.BlockSpec((tm, d), lambda i, f: (i, 0)),
        out_shape=jax.ShapeDtypeStruct((m, d), F32),
        scratch_shapes=[pltpu.VMEM((tm, d), BF16), pltpu.VMEM((tm, d), F32)],
        compiler_params=_cparams(("arbitrary", "arbitrary")),
        name="ffn",
    )(x.reshape(m, d), mod, gains, wg, wu, wd)
    return out.reshape(b, l, d)


def _rope_rotate(x, cos, sin):
    half = x.shape[1] // 2
    swapped = jnp.concatenate(
        [pltpu.roll(x[:, :half], half // 2, axis=1), pltpu.roll(x[:, half:], half // 2, axis=1)], axis=1)
    return x * cos + swapped * sin


def _gla_kernel(*refs, seq, rope, has_s0, want_state):
    it = iter(refs)
    q_ref, k_ref, v_ref, g_ref, lr_ref, wup_ref, bup_ref, gn_ref = [next(it) for _ in range(8)]
    cos_ref = sin_ref = s0_ref = snew_ref = None
    if rope:
        cos_ref, sin_ref = next(it), next(it)
    if has_s0:
        s0_ref = next(it)
    o_ref = next(it)
    if want_state:
        snew_ref = next(it)
    s_ref, oacc_ref, qd_ref, k2_ref, ec_ref = [next(it) for _ in range(5)]

    cs = GLA_CHUNK
    nc = seq // cs
    _, dk, dv = s_ref.shape
    grp = GLA_PREPASS_CHUNKS
    rg = grp * cs
    ri = lax.broadcasted_iota(jnp.int32, (rg, rg), 0)
    ci = lax.broadcasted_iota(jnp.int32, (rg, rg), 1)
    same_chunk = (ri // cs) == (ci // cs)
    masks = (same_chunk & (ri >= ci), same_chunk & (ri <= ci))
    masks_bf = tuple(jnp.where(m, 1.0, 0.0).astype(BF16) for m in masks)

    def prepass(gi, carry):
        rows = pl.ds(pl.multiple_of(gi * rg, rg), rg)
        q = q_ref[rows, :].astype(F32) * (dk ** -0.5)
        k = k_ref[rows, :].astype(F32)
        if rope:
            cos, sin = cos_ref[rows, :], sin_ref[rows, :]
            q = _rope_rotate(q, cos, sin)
            k = _rope_rotate(k, cos, sin)
        v = v_ref[rows, :]
        lr = lr_ref[rows, :].astype(BF16)
        intra = None
        for dirn in range(2):
            la = _log_sigmoid(_dot(lr, wup_ref[dirn]) + bup_ref[dirn]) * (LOG2_E / GLA_TAU)
            b = _dot_exact_rhs(masks_bf[dirn], la)
            last = cs - 1 if dirn == 0 else 0
            totals = [b[j * cs + last:j * cs + last + 1, :] for j in range(grp)]
            b_all = jnp.concatenate([jnp.broadcast_to(t, (cs, dk)) for t in totals], axis=0)
            qd = (q * jnp.exp2(b)).astype(BF16)
            kd = (k * jnp.exp2(-b)).astype(BF16)
            qd_ref[dirn, rows, :] = qd
            k2_ref[dirn, rows, :] = (k * jnp.exp2(b_all - b)).astype(BF16)
            for j in range(grp):
                ec_ref[dirn, gi * grp + j] = _row_to_col(jnp.exp2(totals[j]))
            sc = jnp.where(masks[dirn], _dot_nt(qd, kd), 0.0)
            part = _dot(sc.astype(BF16), v)
            intra = part if intra is None else intra + part
        oacc_ref[rows, :] = intra
        return carry

    lax.fori_loop(0, seq // rg, prepass, 0, unroll=math.gcd(seq // rg, 4))

    def step(c, dirn):
        rows = pl.ds(pl.multiple_of(c * cs, cs), cs)
        s_old = s_ref[dirn]
        inter = _dot(qd_ref[dirn, rows, :], s_old.astype(BF16))
        decay = jnp.concatenate([ec_ref[dirn, c]] * (dv // LANES), axis=1)
        s_ref[dirn] = decay * s_old + _dot_tn(k2_ref[dirn, rows, :], v_ref[rows, :])
        return rows, inter

    for dirn in range(2):
        if has_s0:
            s_ref[dirn] = s0_ref[dirn]
        else:
            s_ref[dirn] = jnp.zeros((dk, dv), F32)

    def first_half(i, carry):
        for dirn, c in ((0, i), (1, nc - 1 - i)):
            rows, inter = step(c, dirn)
            oacc_ref[rows, :] += inter
        return carry

    def second_half(i, carry):
        for dirn, c in ((0, i), (1, nc - 1 - i)):
            rows, inter = step(c, dirn)
            o = _rms(oacc_ref[rows, :] + inter, gn_ref[...]) * _silu(g_ref[rows, :].astype(F32))
            o_ref[rows, :] = o.astype(o_ref.dtype)
        return carry

    half_unroll = math.gcd(nc // 2, 4)
    lax.fori_loop(0, nc // 2, first_half, 0, unroll=half_unroll)
    lax.fori_loop(nc // 2, nc, second_half, 0, unroll=half_unroll)
    if want_state:
        for dirn in range(2):
            snew_ref[dirn] = s_ref[dirn]


def _gla(proj, lr, col, wup, bup, gn, rope_tabs, s0, want_state):
    b, l, _ = proj.shape
    h = GLA_HEADS
    dk = wup.shape[2] // h
    dv = gn.shape[1] // h
    nc = l // GLA_CHUNK
    rope = rope_tabs is not None
    has_s0 = s0 is not None
    in_specs = [
        pl.BlockSpec((None, l, dk), lambda bi, hi: (bi, 0, col["q"] // dk + hi)),
        pl.BlockSpec((None, l, dk), lambda bi, hi: (bi, 0, col["k"] // dk + hi)),
        pl.BlockSpec((None, l, dv), lambda bi, hi: (bi, 0, col["v"] // dv + hi)),
        pl.BlockSpec((None, l, dv), lambda bi, hi: (bi, 0, col["g"] // dv + hi)),
        pl.BlockSpec((None, l, LANES), lambda bi, hi: (bi, 0, 0)),
        pl.BlockSpec((2, LANES, dk), lambda bi, hi: (0, 0, hi)),
        pl.BlockSpec((2, 1, dk), lambda bi, hi: (0, 0, hi)),
        pl.BlockSpec((1, dv), lambda bi, hi: (0, hi)),
    ]
    args = [proj, proj, proj, proj, lr, wup, bup, gn]
    if rope:
        in_specs += [pl.BlockSpec((l, dk), lambda bi, hi: (0, 0), pipeline_mode=pl.Buffered(1))] * 2
        args += list(rope_tabs)
    if has_s0:
        in_specs.append(pl.BlockSpec((None, 2, None, dk, dv), lambda bi, hi: (bi, 0, hi, 0, 0)))
        args.append(s0)
    out_specs = [pl.BlockSpec((None, l, dv), lambda bi, hi: (bi, 0, hi))]
    out_shape = [jax.ShapeDtypeStruct((b, l, h * dv), BF16)]
    if want_state:
        out_specs.append(pl.BlockSpec((None, 2, None, dk, dv), lambda bi, hi: (bi, 0, hi, 0, 0)))
        out_shape.append(jax.ShapeDtypeStruct((b, 2, h, dk, dv), F32))
    res = pl.pallas_call(
        functools.partial(_gla_kernel, seq=l, rope=rope, has_s0=has_s0, want_state=want_state),
        grid=(b, h),
        in_specs=in_specs,
        out_specs=out_specs,
        out_shape=out_shape,
        scratch_shapes=[
            pltpu.VMEM((2, dk, dv), F32), pltpu.VMEM((l, dv), F32),
            pltpu.VMEM((2, l, dk), BF16), pltpu.VMEM((2, l, dk), BF16),
            pltpu.VMEM((2, nc, dk, LANES), F32),
        ],
        compiler_params=_cparams(("arbitrary", "arbitrary")),
        name="gla",
    )(*args)
    return res if want_state else res[0]


def _tile_scan(a, u, reverse):
    n = a.shape[0]
    rid = lax.broadcasted_iota(jnp.int32, a.shape, 0)
    for d in (1, 2, 4):
        shift, m = (n - d, rid < n - d) if reverse else (d, rid >= d)
        a_s = pltpu.roll(a, shift, axis=0)
        u_s = pltpu.roll(u, shift, axis=0)
        u = jnp.where(m, u + a * u_s, u)
        a = jnp.where(m, a * a_s, a)
    return a, u


def _conv4(pad_ref, r0, n, halo, w_ref, b_ref):
    xw = pad_ref[pl.ds(r0, n + 2 * halo), :]
    tot = n + 2 * halo
    return (w_ref[0:1, :] * pltpu.roll(xw, 2, axis=0)[halo:halo + n]
            + w_ref[1:2, :] * pltpu.roll(xw, 1, axis=0)[halo:halo + n]
            + w_ref[2:3, :] * xw[halo:halo + n]
            + w_ref[3:4, :] * pltpu.roll(xw, tot - 1, axis=0)[halo:halo + n]
            + b_ref[...])


def _fill_padded(pad_ref, src_ref, seq, halo, step):
    wd = pad_ref.shape[1]
    pad_ref[0:halo, :] = jnp.zeros((halo, wd), F32)
    pad_ref[seq + halo:seq + 2 * halo, :] = jnp.zeros((halo, wd), F32)

    def body(i, carry):
        r0 = pl.multiple_of(i * step, step)
        pad_ref[pl.ds(r0 + halo, step), :] = src_ref[pl.ds(r0, step), :].astype(F32)
        return carry

    lax.fori_loop(0, seq // step, body, 0)


def _rglru_kernel(*refs, seq, has_h0, want_state, tc):
    it = iter(refs)
    xr_ref, yr_ref, cw_ref, cb_ref, wg_ref, bg_ref, lam_ref = [next(it) for _ in range(7)]
    h0_ref = next(it) if has_h0 else None
    y_ref = next(it)
    hl_ref = next(it) if want_state else None
    xpad_ref, af_ref, uf_ref, ab_ref, ub_ref, gy_ref = [next(it) for _ in range(6)]

    wb = xr_ref.shape[1]
    nblk = wb // RNN_BLOCK
    halo = 8
    _fill_padded(xpad_ref, xr_ref, seq, halo, tc)
    ls = _log_sigmoid(lam_ref[...])
    ls_e2 = (RNN_C * LOG2_E) * ls
    ls_neg = (-RNN_C) * ls

    def gates(c, carry):
        r0 = pl.multiple_of(c * tc, tc)
        rows = pl.ds(r0, tc)
        xc = _conv4(xpad_ref, r0, tc, halo, cw_ref, cb_ref)
        gy_ref[rows, :] = _gelu_tanh(yr_ref[rows, :].astype(F32))
        for j in range(nblk):
            lanes = slice(j * RNN_BLOCK, (j + 1) * RNN_BLOCK)
            xcj = xc[:, lanes]
            gt = _dot(xcj.astype(BF16), wg_ref[j]) + bg_ref[j]
            for dirn, (a_ref, u_ref) in enumerate(((af_ref, uf_ref), (ab_ref, ub_ref))):
                r = _sigmoid(gt[:, dirn * RNN_BLOCK:(dirn + 1) * RNN_BLOCK])
                i = _sigmoid(gt[:, (2 + dirn) * RNN_BLOCK:(3 + dirn) * RNN_BLOCK])
                a = jnp.exp2(r * ls_e2[dirn:dirn + 1, lanes])
                u = jnp.sqrt(jnp.tanh(r * ls_neg[dirn:dirn + 1, lanes]) * (a * a + 1.0)) * i * xcj
                a_ref[rows, lanes] = a
                u_ref[rows, lanes] = u
        return carry

    lax.fori_loop(0, seq // tc, gates, 0)

    h0f = h0_ref[0:1, :] if has_h0 else jnp.zeros((1, wb), F32)
    h0b = h0_ref[1:2, :] if has_h0 else jnp.zeros((1, wb), F32)

    def fwd(t, hc):
        rows = pl.ds(pl.multiple_of(t * 8, 8), 8)
        a, u = _tile_scan(af_ref[rows, :], uf_ref[rows, :], False)
        hh = u + a * hc
        uf_ref[rows, :] = hh
        return hh[7:8, :]

    hlf = lax.fori_loop(0, seq // 8, fwd, h0f, unroll=4)

    n16 = seq // 16

    def bwd(i, hc):
        r16 = pl.multiple_of((n16 - 1 - i) * 16, 16)
        lo, hi = pl.ds(r16, 8), pl.ds(r16 + 8, 8)
        a1, u1 = _tile_scan(ab_ref[hi, :], ub_ref[hi, :], True)
        h1 = u1 + a1 * hc
        a0, u0 = _tile_scan(ab_ref[lo, :], ub_ref[lo, :], True)
        h0 = u0 + a0 * h1[0:1, :]
        y = jnp.concatenate([(uf_ref[lo, :] + h0) * gy_ref[lo, :], (uf_ref[hi, :] + h1) * gy_ref[hi, :]], axis=0)
        y_ref[pl.ds(r16, 16), :] = y.astype(y_ref.dtype)
        return h0[0:1, :]

    hlb = lax.fori_loop(0, n16, bwd, h0b, unroll=2)
    if want_state:
        hl_ref[0:1, :] = hlf
        hl_ref[1:2, :] = hlb


def _rglru(proj, col, cw, cb, wg, bg, lam, h0, want_state, wb=512, tc=256):
    b, l, _ = proj.shape
    w = cw.shape[1]
    tc = min(tc, l)
    nblk = wb // RNN_BLOCK
    has_h0 = h0 is not None
    in_specs = [
        pl.BlockSpec((None, l, wb), lambda bi, j: (bi, 0, col["xr"] // wb + j)),
        pl.BlockSpec((None, l, wb), lambda bi, j: (bi, 0, col["yr"] // wb + j)),
        pl.BlockSpec((4, wb), lambda bi, j: (0, j)),
        pl.BlockSpec((1, wb), lambda bi, j: (0, j)),
        pl.BlockSpec((nblk, RNN_BLOCK, 4 * RNN_BLOCK), lambda bi, j: (j, 0, 0)),
        pl.BlockSpec((nblk, 1, 4 * RNN_BLOCK), lambda bi, j: (j, 0, 0)),
        pl.BlockSpec((2, wb), lambda bi, j: (0, j)),
    ]
    args = [proj, proj, cw, cb, wg, bg, lam]
    if has_h0:
        in_specs.append(pl.BlockSpec((None, 2, wb), lambda bi, j: (bi, 0, j)))
        args.append(h0)
    out_specs = [pl.BlockSpec((None, l, wb), lambda bi, j: (bi, 0, j))]
    out_shape = [jax.ShapeDtypeStruct((b, l, w), BF16)]
    if want_state:
        out_specs.append(pl.BlockSpec((None, 2, wb), lambda bi, j: (bi, 0, j)))
        out_shape.append(jax.ShapeDtypeStruct((b, 2, w), F32))
    res = pl.pallas_call(
        functools.partial(_rglru_kernel, seq=l, has_h0=has_h0, want_state=want_state, tc=tc),
        grid=(b, w // wb),
        in_specs=in_specs,
        out_specs=out_specs,
        out_shape=out_shape,
        scratch_shapes=[pltpu.VMEM((l + 16, wb), F32)] + [pltpu.VMEM((l, wb), F32)] * 5,
        compiler_params=_cparams(("arbitrary", "arbitrary")),
        name="rglru",
    )(*args)
    return res if want_state else res[0]


def _ssd_dt_kernel(dt_ref, bias_ref, alog_ref, cf_ref, cb_ref, qf_ref, qb_ref, cfr_ref, cbr_ref, qfr_ref, qbr_ref,
                   *, seq):
    cs = SSD_CHUNK
    tril, triu = _tri_masks(cs)
    tril_bf = jnp.where(tril, 1.0, 0.0).astype(BF16)
    triu_bf = jnp.where(triu, 1.0, 0.0).astype(BF16)
    a = -jnp.exp(alog_ref[...])

    def body(c, carry):
        rows = pl.ds(pl.multiple_of(c * cs, cs), cs)
        dt = _softplus(dt_ref[rows, :] + bias_ref[...])
        la = dt * a
        log_dt = jnp.log(dt)
        cf = _dot_exact_rhs(tril_bf, la)
        cb = _dot_exact_rhs(triu_bf, la)
        for val, col_ref, row_ref in ((cf, cf_ref, cfr_ref), (cb, cb_ref, cbr_ref),
                                      (cf - log_dt, qf_ref, qfr_ref), (cb - log_dt, qb_ref, qbr_ref)):
            val = val * LOG2_E
            col_ref[rows, :] = val
            row_ref[c] = val.T
        return carry

    lax.fori_loop(0, seq // cs, body, 0)


def _ssd_dt(dt_raw, bias, alog):
    b, l, _ = dt_raw.shape
    nc = l // SSD_CHUNK
    cshp = jax.ShapeDtypeStruct((b, l, LANES), F32)
    cspec = pl.BlockSpec((None, l, LANES), lambda bi: (bi, 0, 0))
    rshp = jax.ShapeDtypeStruct((b, nc, LANES, SSD_CHUNK), F32)
    rspec = pl.BlockSpec((None, nc, LANES, SSD_CHUNK), lambda bi: (bi, 0, 0, 0))
    return pl.pallas_call(
        functools.partial(_ssd_dt_kernel, seq=l),
        grid=(b,),
        in_specs=[
            pl.BlockSpec((None, l, LANES), lambda bi: (bi, 0, 0)),
            pl.BlockSpec((1, LANES), lambda bi: (0, 0)),
            pl.BlockSpec((1, LANES), lambda bi: (0, 0)),
        ],
        out_specs=[cspec] * 4 + [rspec] * 4,
        out_shape=[cshp] * 4 + [rshp] * 4,
        compiler_params=_cparams(("arbitrary",)),
        name="ssd_dt",
    )(dt_raw, bias, alog)


def _ssd_kernel(*refs, seq, has_s0, want_state):
    it = iter(refs)
    (z_ref, x_ref, bm_ref, cm_ref, cfc_ref, cbc_ref, qfc_ref, qbc_ref, cfr_ref, cbr_ref, qfr_ref, qbr_ref,
     cwx_ref, cwb_ref, cwc_ref, cbx_ref, cbb_ref, cbc2_ref, dsk_ref, gn_ref) = [next(it) for _ in range(20)]
    s0_ref = next(it) if has_s0 else None
    y_ref = next(it)
    snew_ref = next(it) if want_state else None
    (xpad_ref, bpad_ref, cpad_ref, xs_ref, bt_ref, cs_ref, cbm_ref, yacc_ref, st_ref) = [next(it) for _ in range(9)]
    cum_cols, cum_rows = (cfc_ref, cbc_ref), (cfr_ref, cbr_ref)
    q_cols, q_rows = (qfc_ref, qbc_ref), (qfr_ref, qbr_ref)

    cs = SSD_CHUNK
    nc = seq // cs
    hp = SSD_HEAD_DIM
    gw = x_ref.shape[1]
    hg = gw // hp
    npair = gw // LANES
    halo = 8
    _fill_padded(xpad_ref, x_ref, seq, halo, cs)
    _fill_padded(bpad_ref, bm_ref, seq, halo, cs)
    _fill_padded(cpad_ref, cm_ref, seq, halo, cs)

    def conv(c, carry):
        r0 = pl.multiple_of(c * cs, cs)
        rows = pl.ds(r0, cs)
        xs_ref[rows, :] = _silu(_conv4(xpad_ref, r0, cs, halo, cwx_ref, cbx_ref))
        bc = _silu(_conv4(bpad_ref, r0, cs, halo, cwb_ref, cbb_ref))
        cc = _silu(_conv4(cpad_ref, r0, cs, halo, cwc_ref, cbc2_ref)).astype(BF16)
        bt_ref[c] = bc.T.astype(BF16)
        cs_ref[rows, :] = cc
        cbm_ref[rows, :] = _dot_nt(cc, bc.astype(BF16))
        return carry

    lax.fori_loop(0, nc, conv, 0, unroll=2)

    masks = _tri_masks(cs)
    lane = lax.broadcasted_iota(jnp.int32, (cs, LANES), 1)
    first_of_pair = lane < hp
    gbase = pl.program_id(1) * (2 * hg)

    def chunk(c, dirn):
        rows = pl.ds(pl.multiple_of(c * cs, cs), cs)
        xs = xs_ref[rows, :]
        c_c = cs_ref[rows, :]
        cbm = cbm_ref[rows, :]
        cum_t = cum_cols[dirn][rows, :]
        q_t = q_cols[dirn][rows, :]
        q_r = q_rows[dirn][c]
        last = cs - 1 if dirn == 0 else 0
        end_lane = cum_t[last:last + 1, :]
        w_all = jnp.exp2(end_lane - q_t)
        e_end = jnp.exp2(jnp.broadcast_to(end_lane, (8, LANES)))
        s_old = st_ref[dirn]
        y_state = _dot(c_c, s_old.astype(BF16))
        tiles, xw_tiles, end_tiles = [], [], []
        for m in range(npair):
            h0 = dirn * hg + 2 * m
            idx = gbase + h0 + lane // hp
            xp = xs[:, m * LANES:(m + 1) * LANES]
            xw_tiles.append((xp * jnp.take_along_axis(w_all, idx, axis=1)).astype(BF16))
            end_tiles.append(jnp.take_along_axis(e_end, idx[0:8], axis=1)[0:1])
            rhs = jnp.concatenate([jnp.where(first_of_pair, xp, 0.0), jnp.where(first_of_pair, 0.0, xp)],
                                  axis=0).astype(BF16)
            lhs, ecum = [], []
            for jj in range(2):
                col = jnp.take_along_axis(cum_t, jnp.zeros_like(lane) + (gbase + h0 + jj), axis=1)
                seg = col - q_r[h0 + jj:h0 + jj + 1, :]
                lhs.append((cbm * jnp.exp2(jnp.where(masks[dirn], seg, -jnp.inf))).astype(BF16))
                ecum.append(jnp.exp2(col))
            y_p = _dot(jnp.concatenate(lhs, axis=1), rhs)
            y_p = y_p + y_state[:, m * LANES:(m + 1) * LANES] * jnp.where(first_of_pair, ecum[0], ecum[1])
            tiles.append(y_p)
        xw = jnp.concatenate(xw_tiles, axis=1)
        st_ref[dirn] = s_old * jnp.concatenate(end_tiles, axis=1) + _dot(bt_ref[c], xw)
        return rows, jnp.concatenate(tiles, axis=1)

    for dirn in range(2):
        if has_s0:
            st_ref[dirn] = s0_ref[dirn].T
        else:
            st_ref[dirn] = jnp.zeros(st_ref.shape[1:], F32)

    def finish(rows, y):
        y = y + dsk_ref[...] * xs_ref[rows, :]
        y = y * _silu(z_ref[rows, :].astype(F32))
        y_ref[rows, :] = _rms(y, gn_ref[...]).astype(y_ref.dtype)

    def first_half(i, carry):
        for dirn, c in ((0, i), (1, nc - 1 - i)):
            rows, y = chunk(c, dirn)
            yacc_ref[rows, :] = y
        return carry

    def second_half(i, carry):
        for dirn, c in ((0, i), (1, nc - 1 - i)):
            rows, y = chunk(c, dirn)
            finish(rows, yacc_ref[rows, :] + y)
        return carry

    half_unroll = math.gcd(nc // 2, 4)
    lax.fori_loop(0, nc // 2, first_half, 0, unroll=half_unroll)
    lax.fori_loop(nc // 2, nc, second_half, 0, unroll=half_unroll)
    if want_state:
        for dirn in range(2):
            snew_ref[dirn] = st_ref[dirn].T


def _ssd(proj, col, cums, cw, cb, dsk, gn, s0, want_state):
    b, l, _ = proj.shape
    g = SSD_GROUPS
    inner = dsk.shape[1]
    gw = inner // g
    ns = SSD_STATE
    nc = l // SSD_CHUNK
    hg2 = 2 * gw // SSD_HEAD_DIM
    has_s0 = s0 is not None
    bcol0 = inner // ns
    ccol0 = bcol0 + g
    assert nc % 2 == 0
    colspec = pl.BlockSpec((None, l, LANES), lambda bi, gi: (bi, 0, 0))
    rowspec = pl.BlockSpec((None, nc, hg2, SSD_CHUNK), lambda bi, gi: (bi, 0, gi, 0))
    in_specs = [
        pl.BlockSpec((None, l, gw), lambda bi, gi: (bi, 0, col["z"] // gw + gi)),
        pl.BlockSpec((None, l, gw), lambda bi, gi: (bi, 0, col["x"] // gw + gi)),
        pl.BlockSpec((None, l, ns), lambda bi, gi: (bi, 0, col["B"] // ns + gi)),
        pl.BlockSpec((None, l, ns), lambda bi, gi: (bi, 0, col["C"] // ns + gi)),
        colspec, colspec, colspec, colspec, rowspec, rowspec, rowspec, rowspec,
        pl.BlockSpec((4, gw), lambda bi, gi: (0, gi)),
        pl.BlockSpec((4, ns), lambda bi, gi: (0, bcol0 + gi)),
        pl.BlockSpec((4, ns), lambda bi, gi: (0, ccol0 + gi)),
        pl.BlockSpec((1, gw), lambda bi, gi: (0, gi)),
        pl.BlockSpec((1, ns), lambda bi, gi: (0, bcol0 + gi)),
        pl.BlockSpec((1, ns), lambda bi, gi: (0, ccol0 + gi)),
        pl.BlockSpec((1, gw), lambda bi, gi: (0, gi)),
        pl.BlockSpec((1, gw), lambda bi, gi: (0, gi)),
    ]
    args = [proj, proj, proj, proj, *cums, cw, cw, cw, cb, cb, cb, dsk, gn]
    if has_s0:
        in_specs.append(pl.BlockSpec((None, 2, gw, ns), lambda bi, gi: (bi, 0, gi, 0)))
        args.append(s0)
    out_specs = [pl.BlockSpec((None, l, gw), lambda bi, gi: (bi, 0, gi))]
    out_shape = [jax.ShapeDtypeStruct((b, l, inner), BF16)]
    if want_state:
        out_specs.append(pl.BlockSpec((None, 2, gw, ns), lambda bi, gi: (bi, 0, gi, 0)))
        out_shape.append(jax.ShapeDtypeStruct((b, 2, inner, ns), F32))
    res = pl.pallas_call(
        functools.partial(_ssd_kernel, seq=l, has_s0=has_s0, want_state=want_state),
        grid=(b, g),
        in_specs=in_specs,
        out_specs=out_specs,
        out_shape=out_shape,
        scratch_shapes=[
            pltpu.VMEM((l + 16, gw), F32), pltpu.VMEM((l + 16, ns), F32), pltpu.VMEM((l + 16, ns), F32),
            pltpu.VMEM((l, gw), F32), pltpu.VMEM((nc, ns, SSD_CHUNK), BF16), pltpu.VMEM((l, ns), BF16),
            pltpu.VMEM((l, SSD_CHUNK), F32), pltpu.VMEM((l, gw), F32), pltpu.VMEM((2, ns, gw), F32),
        ],
        compiler_params=_cparams(("arbitrary", "arbitrary")),
        name="ssd",
    )(*args)
    return res if want_state else res[0]


def _prep_even(w_in_all, layer, w_up, b_up, w_r, b_r, w_i, b_i):
    qk = w_up.shape[2]
    gv = (w_in_all.shape[2] - 2 * qk - 2 * GLA_RANK) // 4
    sizes = (qk, qk, gv, gv, 2 * GLA_RANK, gv, gv)
    offs = [0]
    for s in sizes:
        offs.append(offs[-1] + s)
    w = _even_w_in(w_in_all, layer, offs[4], offs[5])
    w_lr = jnp.pad(w_in_all[layer][:, offs[4]:offs[5]], ((0, 0), (0, LANES - 2 * GLA_RANK))).astype(BF16)
    col = {"q": 0, "k": qk, "v": 2 * qk, "g": 2 * qk + gv, "xr": 2 * qk + 2 * gv, "yr": 2 * qk + 3 * gv}
    wup = jnp.zeros((2, LANES, qk), F32)
    wup = wup.at[0, 0:GLA_RANK].set(w_up[0]).at[1, GLA_RANK:2 * GLA_RANK].set(w_up[1])
    nblk = w_r.shape[1]
    wg = jnp.concatenate([w_r[0], w_r[1], w_i[0], w_i[1]], axis=-1).astype(BF16)
    bg = jnp.concatenate([b.reshape(nblk, 1, RNN_BLOCK) for b in (b_r[0], b_r[1], b_i[0], b_i[1])], axis=-1)
    return {
        "w_in": w, "w_lr": w_lr, "col": col,
        "wup": wup.astype(BF16), "bup": b_up.reshape(2, 1, qk), "wg": wg, "bg": bg,
    }


def _group_major(a, heads):
    g = SSD_GROUPS
    hg = heads // g
    lead = a.shape[:-1]
    return jnp.swapaxes(a.reshape(lead + (2, g, hg)), -3, -2).reshape(lead + (2 * heads,))


def _prep_odd(w_in_all, layer, dt_bias, a_log, d_skip):
    heads = d_skip.shape[0]
    inner = heads * SSD_HEAD_DIM
    xbc = inner + 2 * SSD_GROUPS * SSD_STATE
    dtw = _group_major(_cast_bf16(w_in_all, layer, cols=2 * heads, col_block=(inner + xbc) // (2 * heads)), heads)
    col = {"z": 0, "x": inner, "B": 2 * inner, "C": 2 * inner + SSD_GROUPS * SSD_STATE}
    return {
        "w_in": _cast_bf16(w_in_all, layer, cols=inner + xbc), "w_dt": dtw, "col": col,
        "dt_bias": _group_major(dt_bias.reshape(1, 2 * heads), heads),
        "a_log": _group_major(a_log.reshape(1, 2 * heads), heads),
        "dsk": jnp.repeat(d_skip, SSD_HEAD_DIM).reshape(1, inner),
    }


def _rope_tables(seq, dk):
    nf = dk // 4
    inv = ROPE_BASE ** (-jnp.arange(nf, dtype=F32) / nf)
    t = jnp.arange(seq)
    ang_r = (t // GRID_W).astype(F32)[:, None] * inv
    ang_c = (t % GRID_W).astype(F32)[:, None] * inv
    cos = jnp.concatenate([jnp.cos(ang_r)] * 2 + [jnp.cos(ang_c)] * 2, axis=1)
    sin = jnp.concatenate([-jnp.sin(ang_r), jnp.sin(ang_r), -jnp.sin(ang_c), jnp.sin(ang_c)], axis=1)
    return cos, sin


def _even_mixer(x, mod, gains, p, gla_gn, cw, cb, lam, s0_gla, h0_rnn, rope_tabs, want_state):
    proj, lr = _mm_in(x, mod, gains[0:1], p["w_in"], p["w_lr"], tn=1024)
    gla = _gla(proj, lr, p["col"], p["wup"], p["bup"], gla_gn, rope_tabs, s0_gla, want_state)
    rnn = _rglru(proj, p["col"], cw, cb, p["wg"], p["bg"], lam, h0_rnn, want_state)
    if want_state:
        (o, sg), (yr, sr) = gla, rnn
    else:
        (o, sg), (yr, sr) = (gla, None), (rnn, None)
    y = _mm_out([o, yr], p["w_out"], x, mod, gains[1:2])
    return y, sg, sr


def _odd_mixer(x, mod, gains, p, cw, cb, gn, s0, want_state):
    proj, dt_raw = _mm_in(x, mod, gains[0:1], p["w_in"], p["w_dt"], tn=1024)
    cums = _ssd_dt(dt_raw, p["dt_bias"], p["a_log"])
    res = _ssd(proj, p["col"], cums, cw, cb, p["dsk"], gn, s0, want_state)
    yv, ss = res if want_state else (res, None)
    y = _mm_out([yv], p["w_out"], x, mod, gains[1:2])
    return y, ss


def _stack1(xs):
    return xs[0][:, None] if len(xs) == 1 else jnp.stack(xs, axis=1)


def kernel(x_prompt, x_sample, state_gla, state_rglru, state_ssd, c, c_ctx, w_ada, b_ada, norm_g, ev_w_in, ev_w_out,
           gla_w_up, gla_b_up, gla_norm_g, rnn_conv_w, rnn_conv_b, rnn_w_r, rnn_b_r, rnn_w_i, rnn_b_i, rnn_lam,
           od_w_in, od_w_out, ssd_conv_w, ssd_conv_b, ssd_dt_bias, ssd_a_log, ssd_d, ssd_norm_g,
           ffn_w_gate, ffn_w_up, ffn_w_down):
    depth, d = norm_g.shape[0], norm_g.shape[2]
    nb = c.shape[0]
    rows = -(-(nb + 1) // 8) * 8
    cvec = jnp.zeros((rows, d), F32).at[0].set(c_ctx).at[1:nb + 1].set(c)
    mods = _adaln(cvec, w_ada, b_ada).reshape(depth, rows, MOD_COUNT, d)
    rope_tabs = _rope_tables(x_sample.shape[1], gla_w_up.shape[3] // GLA_HEADS)

    yp, ys = x_prompt, x_sample
    new_gla, new_rnn, new_ssd = [], [], []
    for l in range(depth):
        mod_p, mod_s = mods[l, 0:1], mods[l, 1:nb + 1]
        gains = norm_g[l]
        if l % 2 == 0:
            e = l // 2
            p = _prep_even(ev_w_in, e, gla_w_up[e], gla_b_up[e], rnn_w_r[e], rnn_b_r[e], rnn_w_i[e], rnn_b_i[e])
            p["w_out"] = _cast_bf16(ev_w_out, e)
            gn = gla_norm_g[e].reshape(1, -1)
            cw, cb, lam = rnn_conv_w[e], rnn_conv_b[e].reshape(1, -1), rnn_lam[e]
            yp, sg, sr = _even_mixer(yp, mod_p, gains, p, gn, cw, cb, lam, None, None, None, True)
            ys, _, _ = _even_mixer(ys, mod_s, gains, p, gn, cw, cb, lam, state_gla[:, e], state_rglru[:, e],
                                   rope_tabs, False)
            new_gla.append(sg)
            new_rnn.append(sr)
        else:
            o = l // 2
            p = _prep_odd(od_w_in, o, ssd_dt_bias[o], ssd_a_log[o], ssd_d[o])
            p["w_out"] = _cast_bf16(od_w_out, o)
            cw, cb, gn = ssd_conv_w[o], ssd_conv_b[o].reshape(1, -1), ssd_norm_g[o].reshape(1, -1)
            inner, ns = gn.shape[1], SSD_STATE
            s0 = state_ssd[:, o].reshape(nb, 2, inner, ns)
            yp, ss = _odd_mixer(yp, mod_p, gains, p, cw, cb, gn, None, True)
            ys, _ = _odd_mixer(ys, mod_s, gains, p, cw, cb, gn, s0, False)
            new_ssd.append(ss.reshape(ss.shape[0], 2, inner // SSD_HEAD_DIM, SSD_HEAD_DIM, ns))
        wg, wu, wd = _cast_bf16(ffn_w_gate, l), _cast_bf16(ffn_w_up, l), _cast_bf16(ffn_w_down, l)
        yp = _ffn(yp, mod_p, gains[2:4], wg, wu, wd)
        ys = _ffn(ys, mod_s, gains[2:4], wg, wu, wd)
    return (yp, ys, _stack1(new_gla), _stack1(new_rnn), _stack1(new_ssd))
```

```python
import functools
import math

import jax
import jax.numpy as jnp
from jax import lax
from jax.experimental import pallas as pl
from jax.experimental.pallas import tpu as pltpu

F32 = jnp.float32
BF16 = jnp.bfloat16

EPS = 1e-6
MOD_COUNT = 6
GLA_HEADS = 4
GLA_RANK = 16
GLA_TAU = 16.0
GLA_CHUNK = 64
GLA_PREPASS_CHUNKS = 4
ROPE_BASE = 10000.0
GRID_W = 64
RNN_BLOCK = 128
RNN_C = 8.0
SSD_HEAD_DIM = 64
SSD_STATE = 128
SSD_GROUPS = 8
SSD_CHUNK = 128

LOG2_E = math.log2(math.e)
LANES = 128
V7X_VMEM_BYTES = 64 * 1024 * 1024
VMEM_LIMIT = V7X_VMEM_BYTES - 8 * 1024 * 1024


def _cparams(sem):
    return pltpu.CompilerParams(dimension_semantics=sem, vmem_limit_bytes=VMEM_LIMIT)


def _sigmoid(x):
    return 1.0 / (1.0 + jnp.exp(-x))


def _silu(x):
    h = 0.5 * x
    return h + h * jnp.tanh(h)


def _softplus(x):
    return jnp.maximum(x, 0.0) + jnp.log1p(jnp.exp(-jnp.abs(x)))


def _log_sigmoid(x):
    return jnp.minimum(x, 0.0) - jnp.log1p(jnp.exp(-jnp.abs(x)))


def _gelu_tanh(x):
    return 0.5 * x * (1.0 + jnp.tanh(math.sqrt(2.0 / math.pi) * (x + 0.044715 * (x * x * x))))


def _rms(x, gain):
    ms = jnp.mean(x * x, axis=-1, keepdims=True)
    return x * lax.rsqrt(ms + EPS) * gain


def _split3(x):
    hi = x.astype(BF16)
    r1 = x - hi.astype(F32)
    mid = r1.astype(BF16)
    lo = (r1 - mid.astype(F32)).astype(BF16)
    return hi, mid, lo


def _dot(a, b):
    return jnp.dot(a, b, preferred_element_type=F32)


def _dot_nt(a, b):
    return lax.dot_general(a, b, (((1,), (1,)), ((), ())), preferred_element_type=F32)


def _dot_tn(a, b):
    return lax.dot_general(a, b, (((0,), (0,)), ((), ())), preferred_element_type=F32)


def _dot_exact_rhs(a_bf16, x):
    hi, mid, lo = _split3(x)
    return _dot(a_bf16, hi) + _dot(a_bf16, mid) + _dot(a_bf16, lo)


def _tri_masks(n):
    ri = lax.broadcasted_iota(jnp.int32, (n, n), 0)
    ci = lax.broadcasted_iota(jnp.int32, (n, n), 1)
    return ri >= ci, ri <= ci


def _row_halves(n):
    return (slice(0, n // 2), slice(n // 2, n))


def _row_to_col(row):
    cols = []
    for j in range(row.shape[1] // LANES):
        t = jnp.broadcast_to(row[:, j * LANES:(j + 1) * LANES], (16, LANES)).T
        cols.append(jnp.broadcast_to(t[:, 0:1], (LANES, LANES)))
    return jnp.concatenate(cols, axis=0)


def _cast_kernel(x_ref, o_ref):
    o_ref[...] = x_ref[...].astype(o_ref.dtype)


def _cast_bf16(w, layer, cols=None, col_block=0, block_bytes=6 * 1024 * 1024):
    _, r, n = w.shape
    cols = n if cols is None else cols
    tr = r
    while tr * cols * 4 > block_bytes and tr % 16 == 0:
        tr //= 2
    return pl.pallas_call(
        _cast_kernel,
        grid=(r // tr,),
        in_specs=[pl.BlockSpec((None, tr, cols), lambda i: (layer, i, col_block))],
        out_specs=pl.BlockSpec((tr, cols), lambda i: (i, 0)),
        out_shape=jax.ShapeDtypeStruct((r, cols), BF16),
        compiler_params=_cparams(("arbitrary",)),
        name="cast",
    )(w)


def _transpose_cast_kernel(x_ref, o_ref):
    o_ref[...] = x_ref[0].T.astype(o_ref.dtype)


def _even_w_in(w, layer, lo, hi, tc=512):
    _, d, n = w.shape
    n_out = n - (hi - lo)
    assert lo % tc == 0 and n_out % tc == 0 and hi % 8 == 0
    wt = jnp.swapaxes(w, 1, 2)
    src = lambda j: (layer, pl.multiple_of(j * tc + jnp.where(j < lo // tc, 0, hi - lo), 8), 0)
    return pl.pallas_call(
        _transpose_cast_kernel,
        grid=(n_out // tc,),
        in_specs=[pl.BlockSpec((pl.Element(1), pl.Element(tc), pl.Element(d)), src)],
        out_specs=pl.BlockSpec((d, tc), lambda j: (0, j)),
        out_shape=jax.ShapeDtypeStruct((d, n_out), BF16),
        compiler_params=_cparams(("arbitrary",)),
        name="even_w_in",
    )(wt)


def _adaln_kernel(c_ref, w_ref, b_ref, o_ref):
    s = _silu(c_ref[...]).astype(BF16)
    o_ref[...] = _dot(s, w_ref[...].astype(BF16)) + b_ref[...]


def _adaln(cvec, w_ada, b_ada, tn=1024):
    depth, d, n = w_ada.shape
    r = cvec.shape[0]
    return pl.pallas_call(
        _adaln_kernel,
        grid=(depth, n // tn),
        in_specs=[
            pl.BlockSpec((r, d), lambda l, j: (0, 0)),
            pl.BlockSpec((None, d, tn), lambda l, j: (l, 0, j)),
            pl.BlockSpec((None, 1, tn), lambda l, j: (l, 0, j)),
        ],
        out_specs=pl.BlockSpec((None, r, tn), lambda l, j: (l, 0, j)),
        out_shape=jax.ShapeDtypeStruct((depth, r, n), F32),
        compiler_params=_cparams(("arbitrary", "arbitrary")),
        name="adaln",
    )(cvec, w_ada, b_ada.reshape(depth, 1, n))


def _modnorm(x, mod_ref, gain, j):
    shift = mod_ref[3 * j:3 * j + 1, :]
    scale = mod_ref[3 * j + 1:3 * j + 2, :]
    return _rms(x, gain) * (1.0 + scale) + shift


def _mm_in_kernel(*refs, has_small):
    if has_small:
        x_ref, mod_ref, g_ref, w_ref, ws_ref, o_ref, os_ref, h_ref = refs
    else:
        x_ref, mod_ref, g_ref, w_ref, o_ref, h_ref = refs

    j = pl.program_id(1)

    @pl.when(j == 0)
    def _():
        for r in _row_halves(x_ref.shape[0]):
            h_ref[r, :] = _modnorm(x_ref[r, :], mod_ref, g_ref[...], 0).astype(BF16)
            o_ref[r, :] = _dot(h_ref[r, :], w_ref[...]).astype(o_ref.dtype)
            if has_small:
                os_ref[r, :] = _dot(h_ref[r, :], ws_ref[...].astype(BF16))

    @pl.when(j > 0)
    def _():
        o_ref[...] = _dot(h_ref[...], w_ref[...]).astype(o_ref.dtype)


def _mm_in(x, mod, gain, w, w_small=None, tm=1024, tn=1152):
    b, l, d = x.shape
    n = w.shape[1]
    m = b * l
    nb = mod.shape[0]
    tm = min(tm, l if nb > 1 else m)
    tiles_per_batch = l // tm
    mod_idx = (lambda i, j: (i // tiles_per_batch, 0, 0)) if nb > 1 else (lambda i, j: (0, 0, 0))
    has_small = w_small is not None
    in_specs = [
        pl.BlockSpec((tm, d), lambda i, j: (i, 0)),
        pl.BlockSpec((None, MOD_COUNT, d), mod_idx),
        pl.BlockSpec((1, d), lambda i, j: (0, 0)),
        pl.BlockSpec((d, tn), lambda i, j: (0, j)),
    ]
    args = [x.reshape(m, d), mod, gain, w]
    out_specs = [pl.BlockSpec((tm, tn), lambda i, j: (i, j))]
    out_shape = [jax.ShapeDtypeStruct((m, n), BF16)]
    if has_small:
        ns = w_small.shape[1]
        in_specs.append(pl.BlockSpec((d, ns), lambda i, j: (0, 0)))
        args.append(w_small)
        out_specs.append(pl.BlockSpec((tm, ns), lambda i, j: (i, 0)))
        out_shape.append(jax.ShapeDtypeStruct((m, ns), F32))
    res = pl.pallas_call(
        functools.partial(_mm_in_kernel, has_small=has_small),
        grid=(m // tm, n // tn),
        in_specs=in_specs,
        out_specs=out_specs,
        out_shape=out_shape,
        scratch_shapes=[pltpu.VMEM((tm, d), BF16)],
        compiler_params=_cparams(("arbitrary", "arbitrary")),
        name="mm_in",
    )(*args)
    if has_small:
        return res[0].reshape(b, l, n), res[1].reshape(b, l, -1)
    return res[0].reshape(b, l, n)


def _mm_out_kernel(*refs, n_lhs):
    lhs_refs = refs[:n_lhs]
    w_ref, x_ref, mod_ref, g_ref, o_ref = refs[n_lhs:]
    gate = mod_ref[2:3, :]
    for r in _row_halves(x_ref.shape[0]):
        acc = None
        k0 = 0
        for a_ref in lhs_refs:
            kk = a_ref.shape[1]
            part = _dot(a_ref[r, :].astype(BF16), w_ref[k0:k0 + kk, :])
            acc = part if acc is None else acc + part
            k0 += kk
        o_ref[r, :] = x_ref[r, :] + gate * _rms(acc, g_ref[...])


def _mm_out(lhs_list, w, x, mod, gain, tm=512):
    b, l, d = x.shape
    m = b * l
    nb = mod.shape[0]
    tm = min(tm, l if nb > 1 else m)
    tiles_per_batch = l // tm
    mod_idx = (lambda i: (i // tiles_per_batch, 0, 0)) if nb > 1 else (lambda i: (0, 0, 0))
    ktot = w.shape[0]
    in_specs = [pl.BlockSpec((tm, a.shape[-1]), lambda i: (i, 0)) for a in lhs_list]
    in_specs += [
        pl.BlockSpec((ktot, d), lambda i: (0, 0), pipeline_mode=pl.Buffered(1)),
        pl.BlockSpec((tm, d), lambda i: (i, 0)),
        pl.BlockSpec((None, MOD_COUNT, d), mod_idx),
        pl.BlockSpec((1, d), lambda i: (0, 0)),
    ]
    out = pl.pallas_call(
        functools.partial(_mm_out_kernel, n_lhs=len(lhs_list)),
        grid=(m // tm,),
        in_specs=in_specs,
        out_specs=pl.BlockSpec((tm, d), lambda i: (i, 0)),
        out_shape=jax.ShapeDtypeStruct((m, d), F32),
        compiler_params=_cparams(("arbitrary",)),
        name="mm_out",
    )(*[a.reshape(m, a.shape[-1]) for a in lhs_list], w, x.reshape(m, d), mod, gain)
    return out.reshape(b, l, d)


def _ffn_kernel(x_ref, mod_ref, g_ref, wg_ref, wu_ref, wd_ref, o_ref, h_ref, acc_ref):
    f = pl.program_id(1)
    nf = pl.num_programs(1)
    halves = _row_halves(x_ref.shape[0])

    def down(rows):
        h = h_ref[rows, :]
        a = _silu(_dot(h, wg_ref[...])) * _dot(h, wu_ref[...])
        return _dot(a.astype(BF16), wd_ref[...])

    @pl.when(f == 0)
    def _():
        for r in halves:
            h_ref[r, :] = _modnorm(x_ref[r, :], mod_ref, g_ref[0:1, :], 1).astype(BF16)
            acc_ref[r, :] = down(r)

    @pl.when((f > 0) & (f < nf - 1))
    def _():
        acc_ref[...] += down(slice(None))

    @pl.when(f == nf - 1)
    def _():
        gate = mod_ref[5:6, :]
        for r in halves:
            o_ref[r, :] = x_ref[r, :] + gate * _rms(acc_ref[r, :] + down(r), g_ref[1:2, :])


def _ffn(x, mod, gains, wg, wu, wd, tm=512, tf=512):
    b, l, d = x.shape
    m = b * l
    ff = wg.shape[1]
    nb = mod.shape[0]
    tm = min(tm, l if nb > 1 else m)
    tiles_per_batch = l // tm
    mod_idx = (lambda i, f: (i // tiles_per_batch, 0, 0)) if nb > 1 else (lambda i, f: (0, 0, 0))
    out = pl.pallas_call(
        _ffn_kernel,
        grid=(m // tm, ff // tf),
        in_specs=[
            pl.BlockSpec((tm, d), lambda i, f: (i, 0)),
            pl.BlockSpec((None, MOD_COUNT, d), mod_idx),
            pl.BlockSpec((2, d), lambda i, f: (0, 0)),
            pl.BlockSpec((d, tf), lambda i, f: (0, f)),
            pl.BlockSpec((d, tf), lambda i, f: (0, f)),
            pl.BlockSpec((tf, d), lambda i, f: (f, 0)),
        ],
        out_specs=pl.BlockSpec((tm, d), lambda i, f: (i, 0)),
        out_shape=jax.ShapeDtypeStruct((m, d), F32),
        scratch_shapes=[pltpu.VMEM((tm, d), BF16), pltpu.VMEM((tm, d), F32)],
        compiler_params=_cparams(("arbitrary", "arbitrary")),
        name="ffn",
    )(x.reshape(m, d), mod, gains, wg, wu, wd)
    return out.reshape(b, l, d)


def _rope_rotate(x, cos, sin):
    half = x.shape[1] // 2
    swapped = jnp.concatenate(
        [pltpu.roll(x[:, :half], half // 2, axis=1), pltpu.roll(x[:, half:], half // 2, axis=1)], axis=1)
    return x * cos + swapped * sin


def _gla_kernel(*refs, seq, rope, has_s0, want_state):
    it = iter(refs)
    q_ref, k_ref, v_ref, g_ref, lr_ref, wup_ref, bup_ref, gn_ref = [next(it) for _ in range(8)]
    cos_ref = sin_ref = s0_ref = snew_ref = None
    if rope:
        cos_ref, sin_ref = next(it), next(it)
    if has_s0:
        s0_ref = next(it)
    o_ref = next(it)
    if want_state:
        snew_ref = next(it)
    s_ref, oacc_ref, qd_ref, k2_ref, ec_ref = [next(it) for _ in range(5)]

    cs = GLA_CHUNK
    nc = seq // cs
    _, dk, dv = s_ref.shape
    grp = GLA_PREPASS_CHUNKS
    rg = grp * cs
    ri = lax.broadcasted_iota(jnp.int32, (rg, rg), 0)
    ci = lax.broadcasted_iota(jnp.int32, (rg, rg), 1)
    same_chunk = (ri // cs) == (ci // cs)
    masks = (same_chunk & (ri >= ci), same_chunk & (ri <= ci))
    masks_bf = tuple(jnp.where(m, 1.0, 0.0).astype(BF16) for m in masks)

    def prepass(gi, carry):
        rows = pl.ds(pl.multiple_of(gi * rg, rg), rg)
        q = q_ref[rows, :].astype(F32) * (dk ** -0.5)
        k = k_ref[rows, :].astype(F32)
        if rope:
            cos, sin = cos_ref[rows, :], sin_ref[rows, :]
            q = _rope_rotate(q, cos, sin)
            k = _rope_rotate(k, cos, sin)
        v = v_ref[rows, :]
        lr = lr_ref[rows, :].astype(BF16)
        intra = None
        for dirn in range(2):
            la = _log_sigmoid(_dot(lr, wup_ref[dirn]) + bup_ref[dirn]) * (LOG2_E / GLA_TAU)
            b = _dot_exact_rhs(masks_bf[dirn], la)
            last = cs - 1 if dirn == 0 else 0
            totals = [b[j * cs + last:j * cs + last + 1, :] for j in range(grp)]
            b_all = jnp.concatenate([jnp.broadcast_to(t, (cs, dk)) for t in totals], axis=0)
            qd = (q * jnp.exp2(b)).astype(BF16)
            kd = (k * jnp.exp2(-b)).astype(BF16)
            qd_ref[dirn, rows, :] = qd
            k2_ref[dirn, rows, :] = (k * jnp.exp2(b_all - b)).astype(BF16)
            for j in range(grp):
                ec_ref[dirn, gi * grp + j] = _row_to_col(jnp.exp2(totals[j]))
            sc = jnp.where(masks[dirn], _dot_nt(qd, kd), 0.0)
            part = _dot(sc.astype(BF16), v)
            intra = part if intra is None else intra + part
        oacc_ref[rows, :] = intra
        return carry

    lax.fori_loop(0, seq // rg, prepass, 0, unroll=math.gcd(seq // rg, 4))

    def step(c, dirn):
        rows = pl.ds(pl.multiple_of(c * cs, cs), cs)
        s_old = s_ref[dirn]
        inter = _dot(qd_ref[dirn, rows, :], s_old.astype(BF16))
        decay = jnp.concatenate([ec_ref[dirn, c]] * (dv // LANES), axis=1)
        s_ref[dirn] = decay * s_old + _dot_tn(k2_ref[dirn, rows, :], v_ref[rows, :])
        return rows, inter

    for dirn in range(2):
        if has_s0:
            s_ref[dirn] = s0_ref[dirn]
        else:
            s_ref[dirn] = jnp.zeros((dk, dv), F32)

    def first_half(i, carry):
        for dirn, c in ((0, i), (1, nc - 1 - i)):
            rows, inter = step(c, dirn)
            oacc_ref[rows, :] += inter
        return carry

    def second_half(i, carry):
        for dirn, c in ((0, i), (1, nc - 1 - i)):
            rows, inter = step(c, dirn)
            o = _rms(oacc_ref[rows, :] + inter, gn_ref[...]) * _silu(g_ref[rows, :].astype(F32))
            o_ref[rows, :] = o.astype(o_ref.dtype)
        return carry

    half_unroll = math.gcd(nc // 2, 4)
    lax.fori_loop(0, nc // 2, first_half, 0, unroll=half_unroll)
    lax.fori_loop(nc // 2, nc, second_half, 0, unroll=half_unroll)
    if want_state:
        for dirn in range(2):
            snew_ref[dirn] = s_ref[dirn]


def _gla(proj, lr, col, wup, bup, gn, rope_tabs, s0, want_state):
    b, l, _ = proj.shape
    h = GLA_HEADS
    dk = wup.shape[2] // h
    dv = gn.shape[1] // h
    nc = l // GLA_CHUNK
    rope = rope_tabs is not None
    has_s0 = s0 is not None
    in_specs = [
        pl.BlockSpec((None, l, dk), lambda bi, hi: (bi, 0, col["q"] // dk + hi)),
        pl.BlockSpec((None, l, dk), lambda bi, hi: (bi, 0, col["k"] // dk + hi)),
        pl.BlockSpec((None, l, dv), lambda bi, hi: (bi, 0, col["v"] // dv + hi)),
        pl.BlockSpec((None, l, dv), lambda bi, hi: (bi, 0, col["g"] // dv + hi)),
        pl.BlockSpec((None, l, LANES), lambda bi, hi: (bi, 0, 0)),
        pl.BlockSpec((2, LANES, dk), lambda bi, hi: (0, 0, hi)),
        pl.BlockSpec((2, 1, dk), lambda bi, hi: (0, 0, hi)),
        pl.BlockSpec((1, dv), lambda bi, hi: (0, hi)),
    ]
    args = [proj, proj, proj, proj, lr, wup, bup, gn]
    if rope:
        in_specs += [pl.BlockSpec((l, dk), lambda bi, hi: (0, 0), pipeline_mode=pl.Buffered(1))] * 2
        args += list(rope_tabs)
    if has_s0:
        in_specs.append(pl.BlockSpec((None, 2, None, dk, dv), lambda bi, hi: (bi, 0, hi, 0, 0)))
        args.append(s0)
    out_specs = [pl.BlockSpec((None, l, dv), lambda bi, hi: (bi, 0, hi))]
    out_shape = [jax.ShapeDtypeStruct((b, l, h * dv), BF16)]
    if want_state:
        out_specs.append(pl.BlockSpec((None, 2, None, dk, dv), lambda bi, hi: (bi, 0, hi, 0, 0)))
        out_shape.append(jax.ShapeDtypeStruct((b, 2, h, dk, dv), F32))
    res = pl.pallas_call(
        functools.partial(_gla_kernel, seq=l, rope=rope, has_s0=has_s0, want_state=want_state),
        grid=(b, h),
        in_specs=in_specs,
        out_specs=out_specs,
        out_shape=out_shape,
        scratch_shapes=[
            pltpu.VMEM((2, dk, dv), F32), pltpu.VMEM((l, dv), F32),
            pltpu.VMEM((2, l, dk), BF16), pltpu.VMEM((2, l, dk), BF16),
            pltpu.VMEM((2, nc, dk, LANES), F32),
        ],
        compiler_params=_cparams(("arbitrary", "arbitrary")),
        name="gla",
    )(*args)
    return res if want_state else res[0]


def _tile_scan(a, u, reverse):
    n = a.shape[0]
    rid = lax.broadcasted_iota(jnp.int32, a.shape, 0)
    for d in (1, 2, 4):
        shift, m = (n - d, rid < n - d) if reverse else (d, rid >= d)
        a_s = pltpu.roll(a, shift, axis=0)
        u_s = pltpu.roll(u, shift, axis=0)
        u = jnp.where(m, u + a * u_s, u)
        a = jnp.where(m, a * a_s, a)
    return a, u


def _conv4(pad_ref, r0, n, halo, w_ref, b_ref):
    xw = pad_ref[pl.ds(r0, n + 2 * halo), :]
    tot = n + 2 * halo
    return (w_ref[0:1, :] * pltpu.roll(xw, 2, axis=0)[halo:halo + n]
            + w_ref[1:2, :] * pltpu.roll(xw, 1, axis=0)[halo:halo + n]
            + w_ref[2:3, :] * xw[halo:halo + n]
            + w_ref[3:4, :] * pltpu.roll(xw, tot - 1, axis=0)[halo:halo + n]
            + b_ref[...])


def _fill_padded(pad_ref, src_ref, seq, halo, step):
    wd = pad_ref.shape[1]
    pad_ref[0:halo, :] = jnp.zeros((halo, wd), F32)
    pad_ref[seq + halo:seq + 2 * halo, :] = jnp.zeros((halo, wd), F32)

    def body(i, carry):
        r0 = pl.multiple_of(i * step, step)
        pad_ref[pl.ds(r0 + halo, step), :] = src_ref[pl.ds(r0, step), :].astype(F32)
        return carry

    lax.fori_loop(0, seq // step, body, 0)


def _rglru_kernel(*refs, seq, has_h0, want_state, tc):
    it = iter(refs)
    xr_ref, yr_ref, cw_ref, cb_ref, wg_ref, bg_ref, lam_ref = [next(it) for _ in range(7)]
    h0_ref = next(it) if has_h0 else None
    y_ref = next(it)
    hl_ref = next(it) if want_state else None
    xpad_ref, af_ref, uf_ref, ab_ref, ub_ref, gy_ref = [next(it) for _ in range(6)]

    wb = xr_ref.shape[1]
    nblk = wb // RNN_BLOCK
    halo = 8
    _fill_padded(xpad_ref, xr_ref, seq, halo, tc)
    ls = _log_sigmoid(lam_ref[...])
    ls_e2 = (RNN_C * LOG2_E) * ls
    ls_neg = (-RNN_C) * ls

    def gates(c, carry):
        r0 = pl.multiple_of(c * tc, tc)
        rows = pl.ds(r0, tc)
        xc = _conv4(xpad_ref, r0, tc, halo, cw_ref, cb_ref)
        gy_ref[rows, :] = _gelu_tanh(yr_ref[rows, :].astype(F32))
        for j in range(nblk):
            lanes = slice(j * RNN_BLOCK, (j + 1) * RNN_BLOCK)
            xcj = xc[:, lanes]
            gt = _dot(xcj.astype(BF16), wg_ref[j]) + bg_ref[j]
            for dirn, (a_ref, u_ref) in enumerate(((af_ref, uf_ref), (ab_ref, ub_ref))):
                r = _sigmoid(gt[:, dirn * RNN_BLOCK:(dirn + 1) * RNN_BLOCK])
                i = _sigmoid(gt[:, (2 + dirn) * RNN_BLOCK:(3 + dirn) * RNN_BLOCK])
                a = jnp.exp2(r * ls_e2[dirn:dirn + 1, lanes])
                u = jnp.sqrt(jnp.tanh(r * ls_neg[dirn:dirn + 1, lanes]) * (a * a + 1.0)) * i * xcj
                a_ref[rows, lanes] = a
                u_ref[rows, lanes] = u
        return carry

    lax.fori_loop(0, seq // tc, gates, 0)

    h0f = h0_ref[0:1, :] if has_h0 else jnp.zeros((1, wb), F32)
    h0b = h0_ref[1:2, :] if has_h0 else jnp.zeros((1, wb), F32)

    def fwd(t, hc):
        rows = pl.ds(pl.multiple_of(t * 8, 8), 8)
        a, u = _tile_scan(af_ref[rows, :], uf_ref[rows, :], False)
        hh = u + a * hc
        uf_ref[rows, :] = hh
        return hh[7:8, :]

    hlf = lax.fori_loop(0, seq // 8, fwd, h0f, unroll=4)

    n16 = seq // 16

    def bwd(i, hc):
        r16 = pl.multiple_of((n16 - 1 - i) * 16, 16)
        lo, hi = pl.ds(r16, 8), pl.ds(r16 + 8, 8)
        a1, u1 = _tile_scan(ab_ref[hi, :], ub_ref[hi, :], True)
        h1 = u1 + a1 * hc
        a0, u0 = _tile_scan(ab_ref[lo, :], ub_ref[lo, :], True)
        h0 = u0 + a0 * h1[0:1, :]
        y = jnp.concatenate([(uf_ref[lo, :] + h0) * gy_ref[lo, :], (uf_ref[hi, :] + h1) * gy_ref[hi, :]], axis=0)
        y_ref[pl.ds(r16, 16), :] = y.astype(y_ref.dtype)
        return h0[0:1, :]

    hlb = lax.fori_loop(0, n16, bwd, h0b, unroll=2)
    if want_state:
        hl_ref[0:1, :] = hlf
        hl_ref[1:2, :] = hlb


def _rglru(proj, col, cw, cb, wg, bg, lam, h0, want_state, wb=512, tc=256):
    b, l, _ = proj.shape
    w = cw.shape[1]
    tc = min(tc, l)
    nblk = wb // RNN_BLOCK
    has_h0 = h0 is not None
    in_specs = [
        pl.BlockSpec((None, l, wb), lambda bi, j: (bi, 0, col["xr"] // wb + j)),
        pl.BlockSpec((None, l, wb), lambda bi, j: (bi, 0, col["yr"] // wb + j)),
        pl.BlockSpec((4, wb), lambda bi, j: (0, j)),
        pl.BlockSpec((1, wb), lambda bi, j: (0, j)),
        pl.BlockSpec((nblk, RNN_BLOCK, 4 * RNN_BLOCK), lambda bi, j: (j, 0, 0)),
        pl.BlockSpec((nblk, 1, 4 * RNN_BLOCK), lambda bi, j: (j, 0, 0)),
        pl.BlockSpec((2, wb), lambda bi, j: (0, j)),
    ]
    args = [proj, proj, cw, cb, wg, bg, lam]
    if has_h0:
        in_specs.append(pl.BlockSpec((None, 2, wb), lambda bi, j: (bi, 0, j)))
        args.append(h0)
    out_specs = [pl.BlockSpec((None, l, wb), lambda bi, j: (bi, 0, j))]
    out_shape = [jax.ShapeDtypeStruct((b, l, w), BF16)]
    if want_state:
        out_specs.append(pl.BlockSpec((None, 2, wb), lambda bi, j: (bi, 0, j)))
        out_shape.append(jax.ShapeDtypeStruct((b, 2, w), F32))
    res = pl.pallas_call(
        functools.partial(_rglru_kernel, seq=l, has_h0=has_h0, want_state=want_state, tc=tc),
        grid=(b, w // wb),
        in_specs=in_specs,
        out_specs=out_specs,
        out_shape=out_shape,
        scratch_shapes=[pltpu.VMEM((l + 16, wb), F32)] + [pltpu.VMEM((l, wb), F32)] * 5,
        compiler_params=_cparams(("arbitrary", "arbitrary")),
        name="rglru",
    )(*args)
    return res if want_state else res[0]


def _ssd_dt_kernel(dt_ref, bias_ref, alog_ref, cf_ref, cb_ref, qf_ref, qb_ref, cfr_ref, cbr_ref, qfr_ref, qbr_ref,
                   *, seq):
    cs = SSD_CHUNK
    tril, triu = _tri_masks(cs)
    tril_bf = jnp.where(tril, 1.0, 0.0).astype(BF16)
    triu_bf = jnp.where(triu, 1.0, 0.0).astype(BF16)
    a = -jnp.exp(alog_ref[...])

    def body(c, carry):
        rows = pl.ds(pl.multiple_of(c * cs, cs), cs)
        dt = _softplus(dt_ref[rows, :] + bias_ref[...])
        la = dt * a
        log_dt = jnp.log(dt)
        cf = _dot_exact_rhs(tril_bf, la)
        cb = _dot_exact_rhs(triu_bf, la)
        for val, col_ref, row_ref in ((cf, cf_ref, cfr_ref), (cb, cb_ref, cbr_ref),
                                      (cf - log_dt, qf_ref, qfr_ref), (cb - log_dt, qb_ref, qbr_ref)):
            val = val * LOG2_E
            col_ref[rows, :] = val
            row_ref[c] = val.T
        return carry

    lax.fori_loop(0, seq // cs, body, 0)


def _ssd_dt(dt_raw, bias, alog):
    b, l, _ = dt_raw.shape
    nc = l // SSD_CHUNK
    cshp = jax.ShapeDtypeStruct((b, l, LANES), F32)
    cspec = pl.BlockSpec((None, l, LANES), lambda bi: (bi, 0, 0))
    rshp = jax.ShapeDtypeStruct((b, nc, LANES, SSD_CHUNK), F32)
    rspec = pl.BlockSpec((None, nc, LANES, SSD_CHUNK), lambda bi: (bi, 0, 0, 0))
    return pl.pallas_call(
        functools.partial(_ssd_dt_kernel, seq=l),
        grid=(b,),
        in_specs=[
            pl.BlockSpec((None, l, LANES), lambda bi: (bi, 0, 0)),
            pl.BlockSpec((1, LANES), lambda bi: (0, 0)),
            pl.BlockSpec((1, LANES), lambda bi: (0, 0)),
        ],
        out_specs=[cspec] * 4 + [rspec] * 4,
        out_shape=[cshp] * 4 + [rshp] * 4,
        compiler_params=_cparams(("arbitrary",)),
        name="ssd_dt",
    )(dt_raw, bias, alog)


def _ssd_kernel(*refs, seq, has_s0, want_state):
    it = iter(refs)
    (z_ref, x_ref, bm_ref, cm_ref, cfc_ref, cbc_ref, qfc_ref, qbc_ref, cfr_ref, cbr_ref, qfr_ref, qbr_ref,
     cwx_ref, cwb_ref, cwc_ref, cbx_ref, cbb_ref, cbc2_ref, dsk_ref, gn_ref) = [next(it) for _ in range(20)]
    s0_ref = next(it) if has_s0 else None
    y_ref = next(it)
    snew_ref = next(it) if want_state else None
    (xpad_ref, bpad_ref, cpad_ref, xs_ref, bt_ref, cs_ref, cbm_ref, yacc_ref, st_ref) = [next(it) for _ in range(9)]
    cum_cols, cum_rows = (cfc_ref, cbc_ref), (cfr_ref, cbr_ref)
    q_cols, q_rows = (qfc_ref, qbc_ref), (qfr_ref, qbr_ref)

    cs = SSD_CHUNK
    nc = seq // cs
    hp = SSD_HEAD_DIM
    gw = x_ref.shape[1]
    hg = gw // hp
    npair = gw // LANES
    halo = 8
    _fill_padded(xpad_ref, x_ref, seq, halo, cs)
    _fill_padded(bpad_ref, bm_ref, seq, halo, cs)
    _fill_padded(cpad_ref, cm_ref, seq, halo, cs)

    def conv(c, carry):
        r0 = pl.multiple_of(c * cs, cs)
        rows = pl.ds(r0, cs)
        xs_ref[rows, :] = _silu(_conv4(xpad_ref, r0, cs, halo, cwx_ref, cbx_ref))
        bc = _silu(_conv4(bpad_ref, r0, cs, halo, cwb_ref, cbb_ref))
        cc = _silu(_conv4(cpad_ref, r0, cs, halo, cwc_ref, cbc2_ref)).astype(BF16)
        bt_ref[c] = bc.T.astype(BF16)
        cs_ref[rows, :] = cc
        cbm_ref[rows, :] = _dot_nt(cc, bc.astype(BF16))
        return carry

    lax.fori_loop(0, nc, conv, 0, unroll=2)

    masks = _tri_masks(cs)
    lane = lax.broadcasted_iota(jnp.int32, (cs, LANES), 1)
    first_of_pair = lane < hp
    gbase = pl.program_id(1) * (2 * hg)

    def chunk(c, dirn):
        rows = pl.ds(pl.multiple_of(c * cs, cs), cs)
        xs = xs_ref[rows, :]
        c_c = cs_ref[rows, :]
        cbm = cbm_ref[rows, :]
        cum_t = cum_cols[dirn][rows, :]
        q_t = q_cols[dirn][rows, :]
        q_r = q_rows[dirn][c]
        last = cs - 1 if dirn == 0 else 0
        end_lane = cum_t[last:last + 1, :]
        w_all = jnp.exp2(end_lane - q_t)
        e_end = jnp.exp2(jnp.broadcast_to(end_lane, (8, LANES)))
        s_old = st_ref[dirn]
        y_state = _dot(c_c, s_old.astype(BF16))
        tiles, xw_tiles, end_tiles = [], [], []
        for m in range(npair):
            h0 = dirn * hg + 2 * m
            idx = gbase + h0 + lane // hp
            xp = xs[:, m * LANES:(m + 1) * LANES]
            xw_tiles.append((xp * jnp.take_along_axis(w_all, idx, axis=1)).astype(BF16))
            end_tiles.append(jnp.take_along_axis(e_end, idx[0:8], axis=1)[0:1])
            rhs = jnp.concatenate([jnp.where(first_of_pair, xp, 0.0), jnp.where(first_of_pair, 0.0, xp)],
                                  axis=0).astype(BF16)
            lhs, ecum = [], []
            for jj in range(2):
                col = jnp.take_along_axis(cum_t, jnp.zeros_like(lane) + (gbase + h0 + jj), axis=1)
                seg = col - q_r[h0 + jj:h0 + jj + 1, :]
                lhs.append((cbm * jnp.exp2(jnp.where(masks[dirn], seg, -jnp.inf))).astype(BF16))
                ecum.append(jnp.exp2(col))
            y_p = _dot(jnp.concatenate(lhs, axis=1), rhs)
            y_p = y_p + y_state[:, m * LANES:(m + 1) * LANES] * jnp.where(first_of_pair, ecum[0], ecum[1])
            tiles.append(y_p)
        xw = jnp.concatenate(xw_tiles, axis=1)
        st_ref[dirn] = s_old * jnp.concatenate(end_tiles, axis=1) + _dot(bt_ref[c], xw)
        return rows, jnp.concatenate(tiles, axis=1)

    for dirn in range(2):
        if has_s0:
            st_ref[dirn] = s0_ref[dirn].T
        else:
            st_ref[dirn] = jnp.zeros(st_ref.shape[1:], F32)

    def finish(rows, y):
        y = y + dsk_ref[...] * xs_ref[rows, :]
        y = y * _silu(z_ref[rows, :].astype(F32))
        y_ref[rows, :] = _rms(y, gn_ref[...]).astype(y_ref.dtype)

    def first_half(i, carry):
        for dirn, c in ((0, i), (1, nc - 1 - i)):
            rows, y = chunk(c, dirn)
            yacc_ref[rows, :] = y
        return carry

    def second_half(i, carry):
        for dirn, c in ((0, i), (1, nc - 1 - i)):
            rows, y = chunk(c, dirn)
            finish(rows, yacc_ref[rows, :] + y)
        return carry

    half_unroll = math.gcd(nc // 2, 4)
    lax.fori_loop(0, nc // 2, first_half, 0, unroll=half_unroll)
    lax.fori_loop(nc // 2, nc, second_half, 0, unroll=half_unroll)
    if want_state:
        for dirn in range(2):
            snew_ref[dirn] = st_ref[dirn].T


def _ssd(proj, col, cums, cw, cb, dsk, gn, s0, want_state):
    b, l, _ = proj.shape
    g = SSD_GROUPS
    inner = dsk.shape[1]
    gw = inner // g
    ns = SSD_STATE
    nc = l // SSD_CHUNK
    hg2 = 2 * gw // SSD_HEAD_DIM
    has_s0 = s0 is not None
    bcol0 = inner // ns
    ccol0 = bcol0 + g
    assert nc % 2 == 0
    colspec = pl.BlockSpec((None, l, LANES), lambda bi, gi: (bi, 0, 0))
    rowspec = pl.BlockSpec((None, nc, hg2, SSD_CHUNK), lambda bi, gi: (bi, 0, gi, 0))
    in_specs = [
        pl.BlockSpec((None, l, gw), lambda bi, gi: (bi, 0, col["z"] // gw + gi)),
        pl.BlockSpec((None, l, gw), lambda bi, gi: (bi, 0, col["x"] // gw + gi)),
        pl.BlockSpec((None, l, ns), lambda bi, gi: (bi, 0, col["B"] // ns + gi)),
        pl.BlockSpec((None, l, ns), lambda bi, gi: (bi, 0, col["C"] // ns + gi)),
        colspec, colspec, colspec, colspec, rowspec, rowspec, rowspec, rowspec,
        pl.BlockSpec((4, gw), lambda bi, gi: (0, gi)),
        pl.BlockSpec((4, ns), lambda bi, gi: (0, bcol0 + gi)),
        pl.BlockSpec((4, ns), lambda bi, gi: (0, ccol0 + gi)),
        pl.BlockSpec((1, gw), lambda bi, gi: (0, gi)),
        pl.BlockSpec((1, ns), lambda bi, gi: (0, bcol0 + gi)),
        pl.BlockSpec((1, ns), lambda bi, gi: (0, ccol0 + gi)),
        pl.BlockSpec((1, gw), lambda bi, gi: (0, gi)),
        pl.BlockSpec((1, gw), lambda bi, gi: (0, gi)),
    ]
    args = [proj, proj, proj, proj, *cums, cw, cw, cw, cb, cb, cb, dsk, gn]
    if has_s0:
        in_specs.append(pl.BlockSpec((None, 2, gw, ns), lambda bi, gi: (bi, 0, gi, 0)))
        args.append(s0)
    out_specs = [pl.BlockSpec((None, l, gw), lambda bi, gi: (bi, 0, gi))]
    out_shape = [jax.ShapeDtypeStruct((b, l, inner), BF16)]
    if want_state:
        out_specs.append(pl.BlockSpec((None, 2, gw, ns), lambda bi, gi: (bi, 0, gi, 0)))
        out_shape.append(jax.ShapeDtypeStruct((b, 2, inner, ns), F32))
    res = pl.pallas_call(
        functools.partial(_ssd_kernel, seq=l, has_s0=has_s0, want_state=want_state),
        grid=(b, g),
        in_specs=in_specs,
        out_specs=out_specs,
        out_shape=out_shape,
        scratch_shapes=[
            pltpu.VMEM((l + 16, gw), F32), pltpu.VMEM((l + 16, ns), F32), pltpu.VMEM((l + 16, ns), F32),
            pltpu.VMEM((l, gw), F32), pltpu.VMEM((nc, ns, SSD_CHUNK), BF16), pltpu.VMEM((l, ns), BF16),
            pltpu.VMEM((l, SSD_CHUNK), F32), pltpu.VMEM((l, gw), F32), pltpu.VMEM((2, ns, gw), F32),
        ],
        compiler_params=_cparams(("arbitrary", "arbitrary")),
        name="ssd",
    )(*args)
    return res if want_state else res[0]


def _prep_even(w_in_all, layer, w_up, b_up, w_r, b_r, w_i, b_i):
    qk = w_up.shape[2]
    gv = (w_in_all.shape[2] - 2 * qk - 2 * GLA_RANK) // 4
    sizes = (qk, qk, gv, gv, 2 * GLA_RANK, gv, gv)
    offs = [0]
    for s in sizes:
        offs.append(offs[-1] + s)
    w = _even_w_in(w_in_all, layer, offs[4], offs[5])
    w_lr = jnp.pad(jnp.swapaxes(w_in_all, 1, 2)[layer, offs[4]:offs[5], :].T, ((0, 0), (0, LANES - 2 * GLA_RANK)))
    col = {"q": 0, "k": qk, "v": 2 * qk, "g": 2 * qk + gv, "xr": 2 * qk + 2 * gv, "yr": 2 * qk + 3 * gv}
    wup = jnp.zeros((2, LANES, qk), F32)
    wup = wup.at[0, 0:GLA_RANK].set(w_up[0]).at[1, GLA_RANK:2 * GLA_RANK].set(w_up[1])
    nblk = w_r.shape[1]
    wg = jnp.concatenate([w_r[0], w_r[1], w_i[0], w_i[1]], axis=-1).astype(BF16)
    bg = jnp.concatenate([b.reshape(nblk, 1, RNN_BLOCK) for b in (b_r[0], b_r[1], b_i[0], b_i[1])], axis=-1)
    return {
        "w_in": w, "w_lr": w_lr, "col": col,
        "wup": wup.astype(BF16), "bup": b_up.reshape(2, 1, qk), "wg": wg, "bg": bg,
    }


def _group_major(a, heads):
    g = SSD_GROUPS
    hg = heads // g
    lead = a.shape[:-1]
    return jnp.swapaxes(a.reshape(lead + (2, g, hg)), -3, -2).reshape(lead + (2 * heads,))


def _prep_odd(w_in_all, layer, dt_bias, a_log, d_skip):
    heads = d_skip.shape[0]
    inner = heads * SSD_HEAD_DIM
    xbc = inner + 2 * SSD_GROUPS * SSD_STATE
    dtw = _group_major(_cast_bf16(w_in_all, layer, cols=2 * heads, col_block=(inner + xbc) // (2 * heads)), heads)
    col = {"z": 0, "x": inner, "B": 2 * inner, "C": 2 * inner + SSD_GROUPS * SSD_STATE}
    return {
        "w_in": _cast_bf16(w_in_all, layer, cols=inner + xbc), "w_dt": dtw, "col": col,
        "dt_bias": _group_major(dt_bias.reshape(1, 2 * heads), heads),
        "a_log": _group_major(a_log.reshape(1, 2 * heads), heads),
        "dsk": jnp.repeat(d_skip, SSD_HEAD_DIM).reshape(1, inner),
    }


def _rope_tables(seq, dk):
    nf = dk // 4
    inv = ROPE_BASE ** (-jnp.arange(nf, dtype=F32) / nf)
    t = jnp.arange(seq)
    ang_r = (t // GRID_W).astype(F32)[:, None] * inv
    ang_c = (t % GRID_W).astype(F32)[:, None] * inv
    cos = jnp.concatenate([jnp.cos(ang_r)] * 2 + [jnp.cos(ang_c)] * 2, axis=1)
    sin = jnp.concatenate([-jnp.sin(ang_r), jnp.sin(ang_r), -jnp.sin(ang_c), jnp.sin(ang_c)], axis=1)
    return cos, sin


def _even_mixer(x, mod, gains, p, gla_gn, cw, cb, lam, s0_gla, h0_rnn, rope_tabs, want_state):
    proj, lr = _mm_in(x, mod, gains[0:1], p["w_in"], p["w_lr"], tn=1024)
    gla = _gla(proj, lr, p["col"], p["wup"], p["bup"], gla_gn, rope_tabs, s0_gla, want_state)
    rnn = _rglru(proj, p["col"], cw, cb, p["wg"], p["bg"], lam, h0_rnn, want_state)
    if want_state:
        (o, sg), (yr, sr) = gla, rnn
    else:
        (o, sg), (yr, sr) = (gla, None), (rnn, None)
    y = _mm_out([o, yr], p["w_out"], x, mod, gains[1:2])
    return y, sg, sr


def _odd_mixer(x, mod, gains, p, cw, cb, gn, s0, want_state):
    proj, dt_raw = _mm_in(x, mod, gains[0:1], p["w_in"], p["w_dt"], tn=1024)
    cums = _ssd_dt(dt_raw, p["dt_bias"], p["a_log"])
    res = _ssd(proj, p["col"], cums, cw, cb, p["dsk"], gn, s0, want_state)
    yv, ss = res if want_state else (res, None)
    y = _mm_out([yv], p["w_out"], x, mod, gains[1:2])
    return y, ss


def _stack1(xs):
    return xs[0][:, None] if len(xs) == 1 else jnp.stack(xs, axis=1)


def kernel(x_prompt, x_sample, state_gla, state_rglru, state_ssd, c, c_ctx, w_ada, b_ada, norm_g, ev_w_in, ev_w_out,
           gla_w_up, gla_b_up, gla_norm_g, rnn_conv_w, rnn_conv_b, rnn_w_r, rnn_b_r, rnn_w_i, rnn_b_i, rnn_lam,
           od_w_in, od_w_out, ssd_conv_w, ssd_conv_b, ssd_dt_bias, ssd_a_log, ssd_d, ssd_norm_g,
           ffn_w_gate, ffn_w_up, ffn_w_down):
    depth, d = norm_g.shape[0], norm_g.shape[2]
    nb = c.shape[0]
    rows = -(-(nb + 1) // 8) * 8
    cvec = jnp.zeros((rows, d), F32).at[0].set(c_ctx).at[1:nb + 1].set(c)
    mods = _adaln(cvec, w_ada, b_ada).reshape(depth, rows, MOD_COUNT, d)
    rope_tabs = _rope_tables(x_sample.shape[1], gla_w_up.shape[3] // GLA_HEADS)

    yp, ys = x_prompt, x_sample
    new_gla, new_rnn, new_ssd = [], [], []
    for l in range(depth):
        mod_p, mod_s = mods[l, 0:1], mods[l, 1:nb + 1]
        gains = norm_g[l]
        if l % 2 == 0:
            e = l // 2
            p = _prep_even(ev_w_in, e, gla_w_up[e], gla_b_up[e], rnn_w_r[e], rnn_b_r[e], rnn_w_i[e], rnn_b_i[e])
            p["w_out"] = _cast_bf16(ev_w_out, e)
            gn = gla_norm_g[e].reshape(1, -1)
            cw, cb, lam = rnn_conv_w[e], rnn_conv_b[e].reshape(1, -1), rnn_lam[e]
            yp, sg, sr = _even_mixer(yp, mod_p, gains, p, gn, cw, cb, lam, None, None, None, True)
            ys, _, _ = _even_mixer(ys, mod_s, gains, p, gn, cw, cb, lam, state_gla[:, e], state_rglru[:, e],
                                   rope_tabs, False)
            new_gla.append(sg)
            new_rnn.append(sr)
        else:
            o = l // 2
            p = _prep_odd(od_w_in, o, ssd_dt_bias[o], ssd_a_log[o], ssd_d[o])
            p["w_out"] = _cast_bf16(od_w_out, o)
            cw, cb, gn = ssd_conv_w[o], ssd_conv_b[o].reshape(1, -1), ssd_norm_g[o].reshape(1, -1)
            inner, ns = gn.shape[1], SSD_STATE
            s0 = state_ssd[:, o].reshape(nb, 2, inner, ns)
            yp, ss = _odd_mixer(yp, mod_p, gains, p, cw, cb, gn, None, True)
            ys, _ = _odd_mixer(ys, mod_s, gains, p, cw, cb, gn, s0, False)
            new_ssd.append(ss.reshape(ss.shape[0], 2, inner // SSD_HEAD_DIM, SSD_HEAD_DIM, ns))
        wg, wu, wd = _cast_bf16(ffn_w_gate, l), _cast_bf16(ffn_w_up, l), _cast_bf16(ffn_w_down, l)
        yp = _ffn(yp, mod_p, gains[2:4], wg, wu, wd)
        ys = _ffn(ys, mod_s, gains[2:4], wg, wu, wd)
    return (yp, ys, _stack1(new_gla), _stack1(new_rnn), _stack1(new_ssd))
```
